```python
import math
import jax, jax.numpy as jnp
from jax import lax
import numpy as np

D_MODEL = 4096
BATCH = 4
SEQ = 2048
DEPTH = 4
DEC_BATCH = 8
DEC_SEQ = 1
PAST_LEN = 8192
PAGE_SIZE = 128

ATT_WIDTH = D_MODEL // 2
DIFF_HEAD_DIM = 64
ATT_HEADS = ATT_WIDTH // (2 * DIFF_HEAD_DIM)
KV_HEADS = ATT_HEADS // 2
KV_REP = ATT_HEADS // KV_HEADS
V_HEAD_DIM = 2 * DIFF_HEAD_DIM
SSM_WIDTH = D_MODEL - ATT_WIDTH
SSM_HEAD_DIM = 64
SSM_HEADS = SSM_WIDTH // SSM_HEAD_DIM
SSM_GROUPS = 8
SSM_HPG = SSM_HEADS // SSM_GROUPS
SSM_STATE = 128
CONV_WIDTH = 4
CONV_CH = SSM_WIDTH + 2 * SSM_GROUPS * SSM_STATE
SSD_CHUNK = 128
MIX_WIDTH = ATT_WIDTH + SSM_WIDTH
D_FF = 11008
PLE_DIM = 256
Q_BLOCK = 128
EPS = 1e-6
NEG = -1e30
Q_COLS = ATT_HEADS * 2 * DIFF_HEAD_DIM
K_COLS = KV_HEADS * 2 * DIFF_HEAD_DIM
V_COLS = KV_HEADS * V_HEAD_DIM
Z_COLS = SSM_WIDTH
XBC_COLS = CONV_CH
DT_COLS = SSM_HEADS
IN_COLS = Q_COLS + K_COLS + V_COLS + Z_COLS + XBC_COLS + DT_COLS
IN_SPLITS = [Q_COLS, Q_COLS + K_COLS, Q_COLS + K_COLS + V_COLS, Q_COLS + K_COLS + V_COLS + Z_COLS, Q_COLS + K_COLS + V_COLS + Z_COLS + XBC_COLS]

kernel_name = 'hymba_ssd_diffattn_macaron_step'


def rms_norm(x, g):
    xf = x.astype(jnp.float32)
    y = xf * lax.rsqrt(jnp.mean(xf * xf, axis=-1, keepdims=True) + EPS)
    return (y * g.astype(jnp.float32)).astype(x.dtype)


def swiglu(x, wg, wu, wd):
    return (jax.nn.silu(x @ wg) * (x @ wu)) @ wd


def lambda_init(layer):
    return 0.8 - 0.6 * math.exp(-0.3 * layer)


def diff_attend_block(q, k, v, lam, q_pos, k_pos):
    s = jnp.einsum('bqgrmd,bkgmd->bgrmqk', q.astype(jnp.float32), k.astype(jnp.float32)) * (DIFF_HEAD_DIM ** -0.5)
    mask = k_pos[None, :] <= q_pos[:, None]
    p = jax.nn.softmax(jnp.where(mask, s, NEG), axis=-1)
    a = p[:, :, :, 0] - lam * p[:, :, :, 1]
    o = jnp.einsum('bgrqk,bkge->bqgre', a, v.astype(jnp.float32))
    return o.astype(q.dtype)


def diff_attention(q, k, v, lam, q_pos, k_pos):
    b, sq = q.shape[0], q.shape[1]
    blk = Q_BLOCK if sq % Q_BLOCK == 0 else sq
    nb = sq // blk
    if nb == 1:
        return diff_attend_block(q, k, v, lam, q_pos, k_pos)
    qb = jnp.moveaxis(q.reshape(b, nb, blk, KV_HEADS, KV_REP, 2, DIFF_HEAD_DIM), 1, 0)
    pb = q_pos.reshape(nb, blk)
    ob = lax.map(lambda a: diff_attend_block(a[0], k, v, lam, a[1], k_pos), (qb, pb))
    return jnp.moveaxis(ob, 0, 1).reshape(b, sq, KV_HEADS, KV_REP, V_HEAD_DIM)


def causal_conv(xbc, prev, w, bias):
    xp = jnp.concatenate([prev, xbc], axis=1)
    L = xbc.shape[1]
    y = sum(xp[:, t:t + L] * w[t] for t in range(CONV_WIDTH)) + bias
    return jax.nn.silu(y), xp[:, -(CONV_WIDTH - 1):]


def ssd_chunked(x, dt, A, Bm, Cm):
    b, L = x.shape[0], x.shape[1]
    Q = SSD_CHUNK
    nc = L // Q
    xc = x.reshape(b, nc, Q, SSM_GROUPS, SSM_HPG, SSM_HEAD_DIM)
    dtc = dt.reshape(b, nc, Q, SSM_GROUPS, SSM_HPG)
    Bc = Bm.reshape(b, nc, Q, SSM_GROUPS, SSM_STATE)
    Cc = Cm.reshape(b, nc, Q, SSM_GROUPS, SSM_STATE)
    a_cs = jnp.cumsum(dtc * A, axis=2)
    seg = a_cs[:, :, :, None] - a_cs[:, :, None]
    tril = jnp.tril(jnp.ones((Q, Q), bool))[:, :, None, None]
    Lmat = jnp.exp(jnp.where(tril, seg, -jnp.inf))
    xdt = xc * dtc[..., None]
    CB = jnp.einsum('bcign,bcjgn->bcijg', Cc, Bc)
    y_diag = jnp.einsum('bcijgr,bcjgrp->bcigrp', CB[..., None] * Lmat, xdt)
    decay_to_end = jnp.exp(a_cs[:, :, -1:] - a_cs)
    chunk_states = jnp.einsum('bcjgn,bcjgrp->bcgrpn', Bc, xdt * decay_to_end[..., None])
    chunk_decay = jnp.exp(a_cs[:, :, -1])

    def step(S, inp):
        st, dec = inp
        return dec[..., None, None] * S + st, S

    S0 = jnp.zeros((b, SSM_GROUPS, SSM_HPG, SSM_HEAD_DIM, SSM_STATE), jnp.float32)
    S_fin, S_in = lax.scan(step, S0, (jnp.moveaxis(chunk_states, 1, 0), jnp.moveaxis(chunk_decay, 1, 0)))
    S_in = jnp.moveaxis(S_in, 0, 1)
    y_off = jnp.einsum('bcign,bcgrpn->bcigrp', Cc, S_in) * jnp.exp(a_cs)[..., None]
    y = (y_diag + y_off).reshape(b, L, SSM_GROUPS, SSM_HPG, SSM_HEAD_DIM)
    return y, S_fin


def ssd_recurrent(x, dt, A, Bm, Cm, S0):
    def step(S, inp):
        xt, dtt, Bt, Ct = inp
        S = jnp.exp(dtt * A)[..., None, None] * S + jnp.einsum('bgrp,bgn->bgrpn', xt * dtt[..., None], Bt)
        return S, jnp.einsum('bgrpn,bgn->bgrp', S, Ct)

    S_fin, ys = lax.scan(step, S0, (jnp.moveaxis(x, 1, 0), jnp.moveaxis(dt, 1, 0), jnp.moveaxis(Bm, 1, 0), jnp.moveaxis(Cm, 1, 0)))
    return jnp.moveaxis(ys, 0, 1), S_fin


def token_mixing(xn, w_in, q_g, k_g, lam, lam0, subln_g, conv_w, conv_b, dt_bias, a_log, d_skip, ssm_g, w_out, past_k, past_v, conv_prev, ssm_prev):
    b, s, _ = xn.shape
    h = xn @ w_in
    q, k, v, z, xbc, dt_raw = jnp.split(h, IN_SPLITS, axis=-1)
    qn = rms_norm(q.reshape(b, s, KV_HEADS, KV_REP, 2, DIFF_HEAD_DIM), q_g)
    kn = rms_norm(k.reshape(b, s, KV_HEADS, 2, DIFF_HEAD_DIM), k_g)
    k_rows = kn.reshape(b, s, KV_HEADS, 2 * DIFF_HEAD_DIM)
    v_rows = v.reshape(b, s, KV_HEADS, V_HEAD_DIM)
    if past_k is None:
        start = 0
        k_all, v_all = k_rows, v_rows
    else:
        start = past_k.shape[1]
        k_all = jnp.concatenate([past_k.astype(k_rows.dtype), k_rows], axis=1)
        v_all = jnp.concatenate([past_v.astype(v_rows.dtype), v_rows], axis=1)
    q_pos = start + jnp.arange(s, dtype=jnp.int32)
    k_pos = jnp.arange(start + s, dtype=jnp.int32)
    o = diff_attention(qn, k_all.reshape(b, start + s, KV_HEADS, 2, DIFF_HEAD_DIM), v_all, lam, q_pos, k_pos)
    o = rms_norm(o, subln_g) * (1.0 - lam0)
    att_out = o.reshape(b, s, ATT_WIDTH).astype(xn.dtype)
    if conv_prev is None:
        conv_prev = jnp.zeros((b, CONV_WIDTH - 1, CONV_CH), xbc.dtype)
    xbc_c, conv_new = causal_conv(xbc, conv_prev.astype(xbc.dtype), conv_w, conv_b)
    xbc_c = xbc_c.astype(jnp.float32)
    xs = xbc_c[..., :SSM_WIDTH].reshape(b, s, SSM_GROUPS, SSM_HPG, SSM_HEAD_DIM)
    Bm = xbc_c[..., SSM_WIDTH:SSM_WIDTH + SSM_GROUPS * SSM_STATE].reshape(b, s, SSM_GROUPS, SSM_STATE)
    Cm = xbc_c[..., SSM_WIDTH + SSM_GROUPS * SSM_STATE:].reshape(b, s, SSM_GROUPS, SSM_STATE)
    dt = jax.nn.softplus(dt_raw.astype(jnp.float32) + dt_bias.astype(jnp.float32)).reshape(b, s, SSM_GROUPS, SSM_HPG)
    A = -jnp.exp(a_log.astype(jnp.float32)).reshape(SSM_GROUPS, SSM_HPG)
    if ssm_prev is None:
        y, S_fin = ssd_chunked(xs, dt, A, Bm, Cm)
    else:
        S0 = ssm_prev.astype(jnp.float32).reshape(b, SSM_GROUPS, SSM_HPG, SSM_HEAD_DIM, SSM_STATE)
        y, S_fin = ssd_recurrent(xs, dt, A, Bm, Cm, S0)
    y = y + d_skip.astype(jnp.float32).reshape(SSM_GROUPS, SSM_HPG)[..., None] * xs
    yg = (y.reshape(b, s, SSM_WIDTH) * jax.nn.silu(z.astype(jnp.float32))).reshape(b, s, SSM_GROUPS, SSM_WIDTH // SSM_GROUPS)
    ssm_out = rms_norm(yg, ssm_g.reshape(SSM_GROUPS, SSM_WIDTH // SSM_GROUPS)).reshape(b, s, SSM_WIDTH).astype(xn.dtype)
    out = jnp.concatenate([att_out, ssm_out], axis=-1) @ w_out
    ssm_new = S_fin.reshape(b, SSM_HEADS, SSM_HEAD_DIM, SSM_STATE).astype(xn.dtype)
    return out, k_rows, v_rows, conv_new, ssm_new


def setup_inputs(seed: int = 0) -> dict:
    key = jax.random.key(seed)
    ks = iter(jax.random.split(key, 64))
    n_pages = PAST_LEN // PAGE_SIZE
    n_phys = (5 * DEC_BATCH * n_pages + 3) // 4

    def nrm(shape, scale):
        return jax.random.normal(next(ks), shape, jnp.float32) * scale

    def gain(shape):
        return 1.0 + nrm(shape, 0.05)

    perm = jax.random.permutation(next(ks), n_phys)
    page_table = perm[:DEC_BATCH * n_pages].reshape(DEC_BATCH, n_pages).astype(jnp.int32)
    dt0 = jnp.exp(jax.random.uniform(next(ks), (DEPTH, SSM_HEADS), jnp.float32, math.log(1e-3), math.log(1e-1)))
    dt_bias = dt0 + jnp.log(-jnp.expm1(-dt0))
    a_log = jnp.log(jax.random.uniform(next(ks), (DEPTH, SSM_HEADS), jnp.float32, 1.0, 16.0))
    return {
        'x_prompt': nrm((BATCH, SEQ, D_MODEL), 1.0),
        'x_sample': nrm((DEC_BATCH, DEC_SEQ, D_MODEL), 1.0),
        'cache_k': nrm((DEPTH, n_phys, PAGE_SIZE, KV_HEADS, 2 * DIFF_HEAD_DIM), 1.0),
        'cache_v': nrm((DEPTH, n_phys, PAGE_SIZE, KV_HEADS, V_HEAD_DIM), 1.0),
        'state_conv': nrm((DEPTH, DEC_BATCH, CONV_WIDTH - 1, CONV_CH), 1.0),
        'state_ssm': nrm((DEPTH, DEC_BATCH, SSM_HEADS, SSM_HEAD_DIM, SSM_STATE), 0.5),
        'page_table': page_table,
        'p_prompt': nrm((DEPTH, BATCH, SEQ, PLE_DIM), 1.0),
        'p_sample': nrm((DEPTH, DEC_BATCH, DEC_SEQ, PLE_DIM), 1.0),
        'ffn1_norm': gain((DEPTH, D_MODEL)),
        'ffn1_w_gate': nrm((DEPTH, D_MODEL, D_FF), D_MODEL ** -0.5),
        'ffn1_w_up': nrm((DEPTH, D_MODEL, D_FF), D_MODEL ** -0.5),
        'ffn1_w_down': nrm((DEPTH, D_FF, D_MODEL), D_FF ** -0.5),
        'mix_norm': gain((DEPTH, D_MODEL)),
        'w_in': nrm((DEPTH, D_MODEL, IN_COLS), D_MODEL ** -0.5),
        'q_norm': gain((DEPTH, DIFF_HEAD_DIM)),
        'k_norm': gain((DEPTH, DIFF_HEAD_DIM)),
        'lambda_q1': nrm((DEPTH, DIFF_HEAD_DIM), 0.1),
        'lambda_k1': nrm((DEPTH, DIFF_HEAD_DIM), 0.1),
        'lambda_q2': nrm((DEPTH, DIFF_HEAD_DIM), 0.1),
        'lambda_k2': nrm((DEPTH, DIFF_HEAD_DIM), 0.1),
        'attn_subln': gain((DEPTH, V_HEAD_DIM)),
        'conv_w': nrm((DEPTH, CONV_WIDTH, CONV_CH), CONV_WIDTH ** -0.5),
        'conv_b': nrm((DEPTH, CONV_CH), 0.02),
        'dt_bias': dt_bias,
        'a_log': a_log,
        'd_skip': gain((DEPTH, SSM_HEADS)),
        'ssm_norm': gain((DEPTH, SSM_WIDTH)),
        'w_out': nrm((DEPTH, MIX_WIDTH, D_MODEL), MIX_WIDTH ** -0.5),
        'ffn2_norm': gain((DEPTH, D_MODEL)),
        'ffn2_w_gate': nrm((DEPTH, D_MODEL, D_FF), D_MODEL ** -0.5),
        'ffn2_w_up': nrm((DEPTH, D_MODEL, D_FF), D_MODEL ** -0.5),
        'ffn2_w_down': nrm((DEPTH, D_FF, D_MODEL), D_FF ** -0.5),
        'ple_norm': gain((DEPTH, D_MODEL)),
        'ple_w_gate': nrm((DEPTH, D_MODEL, D_MODEL), D_MODEL ** -0.5),
        'ple_w_proj': nrm((DEPTH, PLE_DIM, D_MODEL), PLE_DIM ** -0.5),
    }


def reference(x_prompt, x_sample, cache_k, cache_v, state_conv, state_ssm, page_table, p_prompt, p_sample,
              ffn1_norm, ffn1_w_gate, ffn1_w_up, ffn1_w_down, mix_norm, w_in, q_norm, k_norm,
              lambda_q1, lambda_k1, lambda_q2, lambda_k2, attn_subln, conv_w, conv_b, dt_bias, a_log, d_skip,
              ssm_norm, w_out, ffn2_norm, ffn2_w_gate, ffn2_w_up, ffn2_w_down, ple_norm, ple_w_gate, ple_w_proj):
    n_pages = page_table.shape[1]
    db = x_sample.shape[0]
    xp, xs = x_prompt, x_sample
    kp_l, vp_l, cp_l, sp_l = [], [], [], []
    ks_l, vs_l, cs_l, ss_l = [], [], [], []
    for i in range(DEPTH):
        lam0 = lambda_init(i)
        lam = (jnp.exp(jnp.sum(lambda_q1[i].astype(jnp.float32) * lambda_k1[i].astype(jnp.float32)))
               - jnp.exp(jnp.sum(lambda_q2[i].astype(jnp.float32) * lambda_k2[i].astype(jnp.float32))) + lam0)
        mix_w = (w_in[i], q_norm[i], k_norm[i], lam, lam0, attn_subln[i], conv_w[i], conv_b[i],
                 dt_bias[i], a_log[i], d_skip[i], ssm_norm[i], w_out[i])
        past_k = cache_k[i][page_table].reshape(db, n_pages * PAGE_SIZE, KV_HEADS, 2 * DIFF_HEAD_DIM)
        past_v = cache_v[i][page_table].reshape(db, n_pages * PAGE_SIZE, KV_HEADS, V_HEAD_DIM)
        xp = xp + 0.5 * swiglu(rms_norm(xp, ffn1_norm[i]), ffn1_w_gate[i], ffn1_w_up[i], ffn1_w_down[i])
        xs = xs + 0.5 * swiglu(rms_norm(xs, ffn1_norm[i]), ffn1_w_gate[i], ffn1_w_up[i], ffn1_w_down[i])
        mp, kp, vp, cp, sp = token_mixing(rms_norm(xp, mix_norm[i]), *mix_w, None, None, None, None)
        ms, ksm, vsm, csm, ssm_ = token_mixing(rms_norm(xs, mix_norm[i]), *mix_w, past_k, past_v, state_conv[i], state_ssm[i])
        xp = xp + mp
        xs = xs + ms
        xp = xp + 0.5 * swiglu(rms_norm(xp, ffn2_norm[i]), ffn2_w_gate[i], ffn2_w_up[i], ffn2_w_down[i])
        xs = xs + 0.5 * swiglu(rms_norm(xs, ffn2_norm[i]), ffn2_w_gate[i], ffn2_w_up[i], ffn2_w_down[i])
        xp = xp + jax.nn.sigmoid(rms_norm(xp, ple_norm[i]) @ ple_w_gate[i]) * (p_prompt[i] @ ple_w_proj[i])
        xs = xs + jax.nn.sigmoid(rms_norm(xs, ple_norm[i]) @ ple_w_gate[i]) * (p_sample[i] @ ple_w_proj[i])
        kp_l.append(kp); vp_l.append(vp); cp_l.append(cp); sp_l.append(sp)
        ks_l.append(ksm); vs_l.append(vsm); cs_l.append(csm); ss_l.append(ssm_)
    k_prompt = jnp.stack(kp_l)
    v_prompt = jnp.stack(vp_l)
    conv_prompt = jnp.stack(cp_l)
    ssm_prompt = jnp.stack(sp_l)
    k_sample = jnp.stack(ks_l)
    v_sample = jnp.stack(vs_l)
    conv_sample = jnp.stack(cs_l)
    ssm_sample = jnp.stack(ss_l)
    return (xp, xs, k_prompt, v_prompt, conv_prompt, ssm_prompt, k_sample, v_sample, conv_sample, ssm_sample)
```

```python
import functools
import math

import jax
import jax.numpy as jnp
from jax import lax
from jax.experimental import pallas as pl
from jax.experimental.pallas import tpu as pltpu

F32 = jnp.float32
BF16 = jnp.bfloat16

D_MODEL = 4096
PAGE_SIZE = 128
HEAD_DIM = 64
KV_HEADS = 8
KV_REP = 2
V_DIM = 128
ATT_WIDTH = KV_HEADS * KV_REP * V_DIM
Q_COLS = KV_HEADS * KV_REP * 2 * HEAD_DIM
K_COLS = KV_HEADS * 2 * HEAD_DIM
V_COLS = KV_HEADS * V_DIM
SSM_WIDTH = 2048
SSM_HEAD_DIM = 64
SSM_HEADS = 32
SSM_GROUPS = 8
SSM_HPG = 4
SSM_STATE = 128
SSM_GROUP_W = SSM_WIDTH // SSM_GROUPS
CONV_WIDTH = 4
CONV_CH = SSM_WIDTH + 2 * SSM_GROUPS * SSM_STATE
SSD_CHUNK = 128
D_FF = 11008
PLE_DIM = 256
EPS = 1e-6
NEG = -1e30
Z_OFF = Q_COLS + K_COLS + V_COLS
X_OFF = Z_OFF + SSM_WIDTH
B_OFF = X_OFF + SSM_WIDTH
C_OFF = B_OFF + SSM_GROUPS * SSM_STATE
DT_OFF = C_OFF + SSM_GROUPS * SSM_STATE
IN_COLS = DT_OFF + SSM_HEADS
IN_COLS_PAD = 10752
PAD_ROWS = 16
LANES = 128
VMEM_LIMIT_BYTES = 56 * 1024 * 1024


def _params(*sem):
    return pltpu.CompilerParams(dimension_semantics=sem, vmem_limit_bytes=VMEM_LIMIT_BYTES)


def _row_tile(m, cap):
    best = 16
    for t in range(16, cap + 1, 16):
        if m % t == 0:
            best = t
    return best


def _split3(x):
    h = x.astype(BF16)
    r = x - h.astype(F32)
    m = r.astype(BF16)
    l = (r - m.astype(F32)).astype(BF16)
    return h, m, l


def _dot(a, b):
    return jnp.dot(a, b, preferred_element_type=F32)


def _dot01_rhs(x, e):
    h, m, l = _split3(x)
    return _dot(h, e) + _dot(m, e) + _dot(l, e)


def _dot01_lhs(e, x):
    h, m, l = _split3(x)
    return _dot(e, h) + _dot(e, m) + _dot(e, l)


def _silu(x):
    return x * jax.nn.sigmoid(x)


def _softplus(x):
    return jnp.maximum(x, 0.0) + jnp.log1p(jnp.exp(-jnp.abs(x)))


def _rmsnorm_kernel(x_ref, g_ref, o_ref):
    x = x_ref[...]
    ms = jnp.mean(x * x, axis=-1, keepdims=True)
    o_ref[...] = (x * lax.rsqrt(ms + EPS) * g_ref[...]).astype(o_ref.dtype)


def _rmsnorm(x, g):
    m, d = x.shape
    tm = _row_tile(m, 320)
    return pl.pallas_call(
        _rmsnorm_kernel,
        grid=(m // tm,),
        in_specs=[pl.BlockSpec((tm, d), lambda i: (i, 0)), pl.BlockSpec((1, d), lambda i: (0, 0))],
        out_specs=pl.BlockSpec((tm, d), lambda i: (i, 0)),
        out_shape=jax.ShapeDtypeStruct((m, d), BF16),
        compiler_params=_params("arbitrary"),
        name="rmsnorm",
    )(x, g.reshape(1, d))


def _mm_kernel(*refs, nx, nw, groups, epi):
    xs, ws, es, o_ref = refs[:nx], refs[nx:nx + nw], refs[nx + nw:-1], refs[-1]
    accs = []
    for grp in groups:
        acc = None
        for xi, wi in grp:
            d = _dot(xs[xi][...], ws[wi][...])
            acc = d if acc is None else acc + d
        accs.append(acc)
    o_ref[...] = epi(*accs, *[e[...] for e in es]).astype(o_ref.dtype)


def _matmul(name, xs, ws, groups, epi, epi_args, n, out_dtype, tn, tm_cap):
    m = xs[0][0].shape[0]
    tm = _row_tile(m, tm_cap)
    in_specs = []
    for _, kb, ki in xs:
        in_specs.append(pl.BlockSpec((tm, kb), lambda j, i, ki=ki: (i, ki)))
    for _, kb, ki in ws:
        in_specs.append(pl.BlockSpec((kb, tn), lambda j, i, ki=ki: (ki, j)))
    for _ in epi_args:
        in_specs.append(pl.BlockSpec((tm, tn), lambda j, i: (i, j)))
    kern = functools.partial(_mm_kernel, nx=len(xs), nw=len(ws), groups=groups, epi=epi)
    return pl.pallas_call(
        kern,
        grid=(n // tn, m // tm),
        in_specs=in_specs,
        out_specs=pl.BlockSpec((tm, tn), lambda j, i: (i, j)),
        out_shape=jax.ShapeDtypeStruct((m, n), out_dtype),
        compiler_params=_params("arbitrary", "arbitrary"),
        name=name,
    )(*[a for a, _, _ in xs], *[a for a, _, _ in ws], *epi_args)


def _epi_swiglu(a, b):
    return _silu(a) * b


def _epi_half_residual(acc, res):
    return res + 0.5 * acc


def _epi_residual(acc, res):
    return res + acc


def _epi_identity(acc):
    return acc


def _epi_gated(a, c, res):
    return res + jax.nn.sigmoid(a) * c


def _ffn(x, g, wg, wu, wd):
    d, f = wg.shape
    xn = _rmsnorm(x, g)
    h = _matmul("ffn_gate_up", [(xn, d, 0)], [(wg, d, 0), (wu, d, 0)], [[(0, 0)], [(0, 1)]], _epi_swiglu, [],
                f, BF16, tn=256, tm_cap=1024)
    return _matmul("ffn_down", [(h, f, 0)], [(wd, f, 0)], [[(0, 0)]], _epi_half_residual, [x],
                   d, F32, tn=512, tm_cap=432)


def _qk_prep_kernel(q_ref, k_ref, qg_ref, kg_ref, bd_ref, qo_ref, ko_ref):
    bd = bd_ref[...]

    def norm(x, g):
        ms = _dot01_rhs(x * x, bd) * (1.0 / HEAD_DIM)
        return x * lax.rsqrt(ms + EPS) * g

    qg = qg_ref[...]
    kg = kg_ref[...]
    for c in range(Q_COLS // LANES):
        sl = slice(c * LANES, (c + 1) * LANES)
        qo_ref[:, sl] = (norm(q_ref[:, sl], qg) * (HEAD_DIM ** -0.5)).astype(qo_ref.dtype)
    for c in range(K_COLS // LANES):
        sl = slice(c * LANES, (c + 1) * LANES)
        ko_ref[:, sl] = norm(k_ref[:, sl], kg)


def _qk_prep(h, q_g, k_g):
    m = h.shape[0]
    tm = _row_tile(m, 320)
    lane = jnp.arange(LANES)
    bd = (lane[:, None] // HEAD_DIM == lane[None, :] // HEAD_DIM).astype(BF16)
    qg2 = jnp.tile(q_g.astype(F32), LANES // HEAD_DIM).reshape(1, LANES)
    kg2 = jnp.tile(k_g.astype(F32), LANES // HEAD_DIM).reshape(1, LANES)
    return pl.pallas_call(
        _qk_prep_kernel,
        grid=(m // tm,),
        in_specs=[
            pl.BlockSpec((tm, Q_COLS), lambda i: (i, 0)),
            pl.BlockSpec((tm, K_COLS), lambda i: (i, Q_COLS // K_COLS)),
            pl.BlockSpec((1, LANES), lambda i: (0, 0)),
            pl.BlockSpec((1, LANES), lambda i: (0, 0)),
            pl.BlockSpec((LANES, LANES), lambda i: (0, 0)),
        ],
        out_specs=[pl.BlockSpec((tm, Q_COLS), lambda i: (i, 0)), pl.BlockSpec((tm, K_COLS), lambda i: (i, 0))],
        out_shape=[jax.ShapeDtypeStruct((m, Q_COLS), BF16), jax.ShapeDtypeStruct((m, K_COLS), F32)],
        compiler_params=_params("arbitrary"),
        name="qk_norm",
    )(h, h, qg2, kg2, bd)


def _lambda(lq1, lk1, lq2, lk2, lam0):
    s1 = jnp.sum(lq1[...] * lk1[...], axis=-1, keepdims=True)
    s2 = jnp.sum(lq2[...] * lk2[...], axis=-1, keepdims=True)
    return jnp.exp(s1) - jnp.exp(s2) + lam0


def _sub_layer_norm(o, g, lam0):
    ms = jnp.mean(o * o, axis=-1, keepdims=True)
    return (o * lax.rsqrt(ms + EPS) * g) * (1.0 - lam0)


def _attn_prompt_kernel(lq1, lk1, lq2, lk2, sg_ref, q_ref, k_ref, v_ref, o_ref, m_sc, l_sc, acc_sc, *, tq, lam0):
    qi = pl.program_id(2)
    q2 = q_ref[...]
    qs = jnp.concatenate([q2[:, :LANES], q2[:, LANES:]], axis=0)
    lane = lax.broadcasted_iota(jnp.int32, qs.shape, 1)
    zero = jnp.zeros_like(qs)
    qm = (jnp.where(lane < HEAD_DIM, qs, zero), jnp.where(lane >= HEAD_DIM, qs, zero))
    m_sc[...] = jnp.full(m_sc.shape, -jnp.inf, F32)
    l_sc[...] = jnp.zeros(l_sc.shape, F32)
    acc_sc[...] = jnp.zeros(acc_sc.shape, F32)

    def block(kj, masked):
        start = pl.multiple_of(kj * tq, tq)
        kb = k_ref[pl.ds(start, tq), :].astype(BF16)
        vb = v_ref[pl.ds(start, tq), :].astype(BF16)
        for m in range(2):
            s = lax.dot_general(qm[m], kb, (((1,), (1,)), ((), ())), preferred_element_type=F32)
            if masked:
                row = lax.broadcasted_iota(jnp.int32, s.shape, 0) % tq
                col = lax.broadcasted_iota(jnp.int32, s.shape, 1)
                s = jnp.where(col <= row, s, NEG)
            m_prev = m_sc[m]
            m_new = jnp.maximum(m_prev, jnp.max(s, axis=-1, keepdims=True))
            alpha = jnp.exp(m_prev - m_new)
            p = jnp.exp(s - m_new)
            l_sc[m] = alpha * l_sc[m] + jnp.sum(p, axis=-1, keepdims=True)
            acc_sc[m] = alpha * acc_sc[m] + _dot(p.astype(BF16), vb)
            m_sc[m] = m_new

    def body(kj, carry):
        block(kj, False)
        return carry

    lax.fori_loop(0, qi, body, 0)
    block(qi, True)

    lam = _lambda(lq1, lk1, lq2, lk2, lam0)
    o = acc_sc[0] / l_sc[0] - lam * (acc_sc[1] / l_sc[1])
    on = _sub_layer_norm(o, sg_ref[...], lam0).astype(o_ref.dtype)
    o_ref[:, :LANES] = on[:tq]
    o_ref[:, LANES:] = on[tq:]


def _lam_specs(nargs):
    zeros = (0,) * 2
    return [pl.BlockSpec((1, HEAD_DIM), lambda *a: zeros) for _ in range(4)] + [pl.BlockSpec((1, V_DIM), lambda *a: zeros)]


def _attn_prompt(qn, kn, h, lam_vecs, subln_g, lam0, batch, seq, m_rows):
    tq = 256 if seq % 256 == 0 else seq
    nq = seq // tq
    kern = functools.partial(_attn_prompt_kernel, tq=tq, lam0=lam0)
    v_blk = (Q_COLS + K_COLS) // V_DIM
    return pl.pallas_call(
        kern,
        grid=(batch, KV_HEADS, nq),
        in_specs=_lam_specs(3) + [
            pl.BlockSpec((tq, 2 * LANES), lambda b, g, i: (b * nq + i, g)),
            pl.BlockSpec((seq, LANES), lambda b, g, i: (b, g)),
            pl.BlockSpec((seq, V_DIM), lambda b, g, i: (b, v_blk + g)),
        ],
        out_specs=pl.BlockSpec((tq, 2 * V_DIM), lambda b, g, i: (b * nq + i, g)),
        out_shape=jax.ShapeDtypeStruct((m_rows, ATT_WIDTH), BF16),
        scratch_shapes=[pltpu.VMEM((2, 2 * tq, 1), F32), pltpu.VMEM((2, 2 * tq, 1), F32),
                        pltpu.VMEM((2, 2 * tq, V_DIM), F32)],
        compiler_params=_params("arbitrary", "arbitrary", "arbitrary"),
        name="attn_prompt",
    )(*lam_vecs, subln_g.reshape(1, V_DIM), qn, kn, h)


def _attn_decode_kernel(pt_ref, lq1, lk1, lq2, lk2, sg_ref, q_ref, kn_ref, vn_ref, ck_ref, cv_ref, o_ref,
                        m_sc, l_sc, acc_sc, *, lam0):
    p = pl.program_id(1)
    lane = lax.broadcasted_iota(jnp.int32, (1, LANES), 1)
    lo = lane < HEAD_DIM
    zrow = jnp.zeros((1, LANES), F32)

    def q_rows(g):
        a = q_ref[0, :, g * 2 * LANES:g * 2 * LANES + LANES]
        b = q_ref[0, :, g * 2 * LANES + LANES:(g + 1) * 2 * LANES]
        return jnp.concatenate([jnp.where(lo, a, zrow), jnp.where(lo, zrow, a),
                                jnp.where(lo, b, zrow), jnp.where(lo, zrow, b),
                                zrow, zrow, zrow, zrow], axis=0)

    @pl.when(p == 0)
    def _():
        m_sc[...] = jnp.full(m_sc.shape, -jnp.inf, F32)
        l_sc[...] = jnp.zeros(l_sc.shape, F32)
        acc_sc[...] = jnp.zeros(acc_sc.shape, F32)

    def update(g, s, v_fn):
        m_prev = m_sc[g]
        m_new = jnp.maximum(m_prev, jnp.max(s, axis=-1, keepdims=True))
        alpha = jnp.exp(m_prev - m_new)
        pr = jnp.exp(s - m_new)
        l_sc[g] = alpha * l_sc[g] + jnp.sum(pr, axis=-1, keepdims=True)
        acc_sc[g] = alpha * acc_sc[g] + v_fn(pr)
        m_sc[g] = m_new

    for g in range(KV_HEADS):
        qg = q_rows(g).astype(BF16)
        kg = ck_ref[0, 0, pl.ds(g, PAGE_SIZE, stride=KV_HEADS), :].astype(BF16)
        vg = cv_ref[0, 0, pl.ds(g, PAGE_SIZE, stride=KV_HEADS), :].astype(BF16)
        s = lax.dot_general(qg, kg, (((1,), (1,)), ((), ())), preferred_element_type=F32)
        update(g, s, lambda pr, vg=vg: _dot(pr.astype(BF16), vg))

    @pl.when(p == pl.num_programs(1) - 1)
    def _():
        lam = _lambda(lq1, lk1, lq2, lk2, lam0)
        sg = sg_ref[...]
        for g in range(KV_HEADS):
            k_new = kn_ref[0, :, g * LANES:(g + 1) * LANES]
            v_new = vn_ref[0, :, g * V_DIM:(g + 1) * V_DIM]
            s = jnp.sum(q_rows(g) * k_new, axis=-1, keepdims=True)
            update(g, s, lambda pr, v_new=v_new: pr * v_new)
            o = acc_sc[g] / l_sc[g]
            for r in range(KV_REP):
                orow = o[2 * r:2 * r + 1] - lam * o[2 * r + 1:2 * r + 2]
                col = (g * KV_REP + r) * V_DIM
                o_ref[0, :, col:col + V_DIM] = _sub_layer_norm(orow, sg, lam0)


def _attn_decode(qs, ks, vs, cache_k, cache_v, page_table, layer, lam_vecs, subln_g, lam0):
    db, n_pages = page_table.shape
    depth, n_phys = cache_k.shape[:2]
    ck = cache_k.reshape(depth, n_phys, PAGE_SIZE * KV_HEADS, 2 * HEAD_DIM)
    cv = cache_v.reshape(depth, n_phys, PAGE_SIZE * KV_HEADS, V_DIM)
    kern = functools.partial(_attn_decode_kernel, lam0=lam0)
    z2 = lambda b, p, pt: (0, 0)
    row = lambda b, p, pt: (b, 0, 0)
    page = lambda b, p, pt: (layer, pt[b, p], 0, 0)
    grid_spec = pltpu.PrefetchScalarGridSpec(
        num_scalar_prefetch=1,
        grid=(db, n_pages),
        in_specs=[pl.BlockSpec((1, HEAD_DIM), z2) for _ in range(4)] + [
            pl.BlockSpec((1, V_DIM), z2),
            pl.BlockSpec((1, 1, Q_COLS), row),
            pl.BlockSpec((1, 1, K_COLS), row),
            pl.BlockSpec((1, 1, V_COLS), row),
            pl.BlockSpec((1, 1, PAGE_SIZE * KV_HEADS, 2 * HEAD_DIM), page),
            pl.BlockSpec((1, 1, PAGE_SIZE * KV_HEADS, V_DIM), page),
        ],
        out_specs=pl.BlockSpec((1, 1, ATT_WIDTH), row),
        scratch_shapes=[pltpu.VMEM((KV_HEADS, 8, 1), F32), pltpu.VMEM((KV_HEADS, 8, 1), F32),
                        pltpu.VMEM((KV_HEADS, 8, V_DIM), F32)],
    )
    return pl.pallas_call(
        kern,
        grid_spec=grid_spec,
        out_shape=jax.ShapeDtypeStruct((db, 1, ATT_WIDTH), F32),
        compiler_params=_params("arbitrary", "arbitrary"),
        name="attn_decode",
    )(page_table, *lam_vecs, subln_g.reshape(1, V_DIM), qs, ks, vs, ck, cv)


def _gate_norm(y, xs_c, z, d_e, g_e):
    y = (y + d_e * xs_c) * _silu(z)
    outs = []
    for g in range(SSM_GROUPS):
        yg = y[:, g * SSM_GROUP_W:(g + 1) * SSM_GROUP_W]
        ms = jnp.mean(yg * yg, axis=-1, keepdims=True)
        outs.append(yg * lax.rsqrt(ms + EPS) * g_e[:, g * SSM_GROUP_W:(g + 1) * SSM_GROUP_W])
    return outs


def _ssd_prompt_kernel(z_ref, x_ref, b_ref, c_ref, dt_ref, cwx, cwb, cwc, cbx, cbb, cbc, dtb_e, alog_e, dtb_c, alog_c,
                       d_e, g_e, tril_ref, exp_ref, o_ref, s_ref, xpx, xpb, xpc, st_sc, y_sc):
    c = pl.program_id(1)
    q = SSD_CHUNK

    @pl.when(c == 0)
    def _():
        xpx[0:8, :] = jnp.zeros((8, xpx.shape[1]), F32)
        xpb[0:8, :] = jnp.zeros((8, xpb.shape[1]), F32)
        xpc[0:8, :] = jnp.zeros((8, xpc.shape[1]), F32)
        st_sc[...] = jnp.zeros(st_sc.shape, F32)

    def conv(src, pad, w, b):
        pad[8:8 + q, :] = src[...]
        y = b[...]
        for t in range(CONV_WIDTH):
            y = y + w[t:t + 1, :] * pad[5 + t:5 + t + q, :]
        tail = pad[q:q + 8, :]
        pad[0:8, :] = tail
        return _silu(y)

    xs_c = conv(x_ref, xpx, cwx, cbx)
    bc = conv(b_ref, xpb, cwb, cbb)
    cc = conv(c_ref, xpc, cwc, cbc)

    tril = tril_ref[...]
    dtr = dt_ref[...]
    dt_c = _softplus(dtr + dtb_c[...])
    acs_c = _dot01_lhs(tril, dt_c * (-jnp.exp(alog_c[...])))
    acs_t = acs_c.T
    dt_e = _softplus(_dot01_rhs(dtr, exp_ref[...]) + dtb_e[...])
    acs_e = _dot01_lhs(tril, dt_e * (-jnp.exp(alog_e[...])))
    a_last = acs_e[q - 1:q, :]
    xdt = xs_c * dt_e
    xdtd = (xdt * jnp.exp(a_last - acs_e)).astype(BF16)
    xdt = xdt.astype(BF16)
    ea = jnp.exp(acs_e)
    dec_end = jnp.exp(a_last)

    ii = lax.broadcasted_iota(jnp.int32, (q, q), 0)
    jj = lax.broadcasted_iota(jnp.int32, (q, q), 1)
    causal = jj <= ii
    lane_lo = jj < SSM_HEAD_DIM

    for g in range(SSM_GROUPS):
        gs = slice(g * SSM_GROUP_W, (g + 1) * SSM_GROUP_W)
        bg = bc[:, g * SSM_STATE:(g + 1) * SSM_STATE].astype(BF16)
        cg = cc[:, g * SSM_STATE:(g + 1) * SSM_STATE].astype(BF16)
        cb = lax.dot_general(cg, bg, (((1,), (1,)), ((), ())), preferred_element_type=F32)
        st = st_sc[g]
        y_off = _dot(cg, st.astype(BF16)) * ea[:, gs]
        pairs = []
        for pr in range(SSM_HPG // 2):
            col = g * SSM_GROUP_W + pr * LANES
            xpair = xdt[:, col:col + LANES]
            ys = []
            for hh in range(2):
                head = g * SSM_HPG + pr * 2 + hh
                seg = acs_c[:, head:head + 1] - acs_t[head:head + 1, :]
                lmat = jnp.where(causal, jnp.exp(seg), 0.0)
                ys.append(_dot((cb * lmat).astype(BF16), xpair))
            pairs.append(jnp.where(lane_lo, ys[0], ys[1]))
        y_sc[:, gs] = jnp.concatenate(pairs, axis=1) + y_off
        upd = lax.dot_general(bg, xdtd[:, gs], (((0,), (0,)), ((), ())), preferred_element_type=F32)
        st_sc[g] = st * dec_end[:, gs] + upd

    outs = _gate_norm(y_sc[...], xs_c, z_ref[...], d_e[...], g_e[...])
    for g in range(SSM_GROUPS):
        o_ref[:, g * SSM_GROUP_W:(g + 1) * SSM_GROUP_W] = outs[g].astype(o_ref.dtype)

    @pl.when(c == pl.num_programs(1) - 1)
    def _():
        for g in range(SSM_GROUPS):
            t = st_sc[g].T
            for r in range(SSM_HPG):
                s_ref[0, g * SSM_HPG + r] = t[r * SSM_HEAD_DIM:(r + 1) * SSM_HEAD_DIM, :]


def _ssm_consts(conv_w, conv_b, dt_bias, a_log, d_skip, ssm_g):
    rep = lambda v: jnp.repeat(v.astype(F32), SSM_HEAD_DIM).reshape(1, SSM_WIDTH)
    pad = lambda v: jnp.pad(v.astype(F32), (0, LANES - SSM_HEADS)).reshape(1, LANES)
    head_of_col = jnp.arange(SSM_WIDTH) // SSM_HEAD_DIM
    expand = (jnp.arange(LANES)[:, None] == head_of_col[None, :]).astype(BF16)
    tril = (jnp.arange(SSD_CHUNK)[None, :] <= jnp.arange(SSD_CHUNK)[:, None]).astype(BF16)
    return dict(conv_w=conv_w.astype(F32), conv_b=conv_b.astype(F32).reshape(1, CONV_CH),
                dtb_e=rep(dt_bias), alog_e=rep(a_log), dtb_c=pad(dt_bias), alog_c=pad(a_log),
                d_e=rep(d_skip), g_e=ssm_g.astype(F32).reshape(1, SSM_WIDTH), tril=tril, expand=expand)


def _conv_param_specs(zmap):
    n_x, n_b = SSM_WIDTH, SSM_GROUPS * SSM_STATE
    cmap = lambda k: (lambda *a: (0, k))
    return [pl.BlockSpec((CONV_WIDTH, n_x), cmap(0)), pl.BlockSpec((CONV_WIDTH, n_b), cmap(2)),
            pl.BlockSpec((CONV_WIDTH, n_b), cmap(3)),
            pl.BlockSpec((1, n_x), cmap(0)), pl.BlockSpec((1, n_b), cmap(2)), pl.BlockSpec((1, n_b), cmap(3))]


def _ssd_prompt(h, sc, batch, seq, m_rows):
    q = SSD_CHUNK
    nc = seq // q
    rb = lambda b, c: b * nc + c
    n_b = SSM_GROUPS * SSM_STATE
    zmap = lambda b, c: (0, 0)
    vec = lambda n: pl.BlockSpec((1, n), zmap)
    in_specs = [
        pl.BlockSpec((q, SSM_WIDTH), lambda b, c: (rb(b, c), Z_OFF // SSM_WIDTH)),
        pl.BlockSpec((q, SSM_WIDTH), lambda b, c: (rb(b, c), X_OFF // SSM_WIDTH)),
        pl.BlockSpec((q, n_b), lambda b, c: (rb(b, c), B_OFF // n_b)),
        pl.BlockSpec((q, n_b), lambda b, c: (rb(b, c), C_OFF // n_b)),
        pl.BlockSpec((q, LANES), lambda b, c: (rb(b, c), DT_OFF // LANES)),
    ] + _conv_param_specs(zmap) + [
        vec(SSM_WIDTH), vec(SSM_WIDTH), vec(LANES), vec(LANES), vec(SSM_WIDTH), vec(SSM_WIDTH),
        pl.BlockSpec((q, q), zmap), pl.BlockSpec((LANES, SSM_WIDTH), zmap),
    ]
    out, state = pl.pallas_call(
        _ssd_prompt_kernel,
        grid=(batch, nc),
        in_specs=in_specs,
        out_specs=[pl.BlockSpec((q, SSM_WIDTH), lambda b, c: (rb(b, c), 0)),
                   pl.BlockSpec((1, SSM_HEADS, SSM_HEAD_DIM, SSM_STATE), lambda b, c: (b, 0, 0, 0))],
        out_shape=[jax.ShapeDtypeStruct((m_rows, SSM_WIDTH), BF16),
                   jax.ShapeDtypeStruct((batch, SSM_HEADS, SSM_HEAD_DIM, SSM_STATE), F32)],
        scratch_shapes=[pltpu.VMEM((q + 8, SSM_WIDTH), F32), pltpu.VMEM((q + 8, n_b), F32),
                        pltpu.VMEM((q + 8, n_b), F32),
                        pltpu.VMEM((SSM_GROUPS, SSM_STATE, SSM_GROUP_W), F32), pltpu.VMEM((q, SSM_WIDTH), F32)],
        compiler_params=_params("arbitrary", "arbitrary"),
        name="ssd_prompt",
    )(h, h, h, h, h, sc["conv_w"], sc["conv_w"], sc["conv_w"], sc["conv_b"], sc["conv_b"], sc["conv_b"],
      sc["dtb_e"], sc["alog_e"], sc["dtb_c"], sc["alog_c"], sc["d_e"], sc["g_e"], sc["tril"], sc["expand"])
    return out, state


def _ssm_decode_pre_kernel(x_ref, b_ref, c_ref, dt_ref, prev_ref, cwx, cwb, cwc, cbx, cbb, cbc, dtb_e, dtb_c, alog_c,
                           exp_ref, xs_o, b_o, c_o, xdt_o, dec_o):
    def conv(src, col, w, b):
        n = src.shape[1]
        y = b[...] + w[CONV_WIDTH - 1:CONV_WIDTH, :] * src[...]
        for t in range(CONV_WIDTH - 1):
            y = y + w[t:t + 1, :] * prev_ref[t, :, col:col + n]
        return _silu(y)

    xs_c = conv(x_ref, 0, cwx, cbx)
    xs_o[...] = xs_c
    b_o[...] = conv(b_ref, SSM_WIDTH, cwb, cbb)
    c_o[...] = conv(c_ref, SSM_WIDTH + SSM_GROUPS * SSM_STATE, cwc, cbc)
    dtr = dt_ref[...]
    dt_c = _softplus(dtr + dtb_c[...])
    dec_o[...] = jnp.exp(dt_c * (-jnp.exp(alog_c[...])))
    dt_e = _softplus(_dot01_rhs(dtr, exp_ref[...]) + dtb_e[...])
    xdt_o[...] = xs_c * dt_e


def _ssm_decode_state_kernel(s_ref, xdt_ref, dec_ref, b_ref, c_ref, so_ref, y_ref):
    for head in range(SSM_HEADS):
        g = head // SSM_HPG
        s = dec_ref[0, head] * s_ref[0, 0, head] + xdt_ref[0, head] * b_ref[0, g]
        so_ref[0, head] = s
        y_ref[0, head] = jnp.sum(s * c_ref[0, g], axis=-1, keepdims=True)


def _ssm_decode_post_kernel(y_ref, xs_ref, z_ref, d_e, g_e, o_ref):
    outs = _gate_norm(y_ref[...], xs_ref[...], z_ref[...], d_e[...], g_e[...])
    o_ref[...] = jnp.zeros(o_ref.shape, o_ref.dtype)
    db = y_ref.shape[0]
    for g in range(SSM_GROUPS):
        o_ref[0:db, g * SSM_GROUP_W:(g + 1) * SSM_GROUP_W] = outs[g]


def _ssm_decode(h, sc, state_conv_l, state_ssm, layer, row0, db):
    n_b = SSM_GROUPS * SSM_STATE
    rblk = row0 // db
    zmap = lambda i: (0, 0)
    vec = lambda n: pl.BlockSpec((1, n), zmap)
    prev = jnp.transpose(state_conv_l, (1, 0, 2)).astype(F32)
    f = lambda n: jax.ShapeDtypeStruct((db, n), F32)
    xs_c, b_c, c_c, xdt, dec = pl.pallas_call(
        _ssm_decode_pre_kernel,
        grid=(1,),
        in_specs=[
            pl.BlockSpec((db, SSM_WIDTH), lambda i: (rblk, X_OFF // SSM_WIDTH)),
            pl.BlockSpec((db, n_b), lambda i: (rblk, B_OFF // n_b)),
            pl.BlockSpec((db, n_b), lambda i: (rblk, C_OFF // n_b)),
            pl.BlockSpec((db, LANES), lambda i: (rblk, DT_OFF // LANES)),
            pl.BlockSpec((CONV_WIDTH - 1, db, CONV_CH), lambda i: (0, 0, 0)),
        ] + _conv_param_specs(zmap) + [vec(SSM_WIDTH), vec(LANES), vec(LANES),
                                       pl.BlockSpec((LANES, SSM_WIDTH), zmap)],
        out_specs=[pl.BlockSpec((db, SSM_WIDTH), zmap), pl.BlockSpec((db, n_b), zmap), pl.BlockSpec((db, n_b), zmap),
                   pl.BlockSpec((db, SSM_WIDTH), zmap), pl.BlockSpec((db, LANES), zmap)],
        out_shape=[f(SSM_WIDTH), f(n_b), f(n_b), f(SSM_WIDTH), f(LANES)],
        compiler_params=_params("arbitrary"),
        name="ssm_decode_pre",
    )(h, h, h, h, prev, sc["conv_w"], sc["conv_w"], sc["conv_w"], sc["conv_b"], sc["conv_b"], sc["conv_b"],
      sc["dtb_e"], sc["dtb_c"], sc["alog_c"], sc["expand"])

    xdt4 = xdt.reshape(db, SSM_HEADS, SSM_HEAD_DIM, 1)
    dec4 = dec[:, :SSM_HEADS].reshape(db, SSM_HEADS, 1, 1)
    b4 = b_c.reshape(db, SSM_GROUPS, 1, SSM_STATE)
    c4 = c_c.reshape(db, SSM_GROUPS, 1, SSM_STATE)
    s_new, y4 = pl.pallas_call(
        _ssm_decode_state_kernel,
        grid=(db,),
        in_specs=[
            pl.BlockSpec((1, 1, SSM_HEADS, SSM_HEAD_DIM, SSM_STATE), lambda b: (layer, b, 0, 0, 0)),
            pl.BlockSpec((1, SSM_HEADS, SSM_HEAD_DIM, 1), lambda b: (b, 0, 0, 0)),
            pl.BlockSpec((1, SSM_HEADS, 1, 1), lambda b: (b, 0, 0, 0)),
            pl.BlockSpec((1, SSM_GROUPS, 1, SSM_STATE), lambda b: (b, 0, 0, 0)),
            pl.BlockSpec((1, SSM_GROUPS, 1, SSM_STATE), lambda b: (b, 0, 0, 0)),
        ],
        out_specs=[pl.BlockSpec((1, SSM_HEADS, SSM_HEAD_DIM, SSM_STATE), lambda b: (b, 0, 0, 0)),
                   pl.BlockSpec((1, SSM_HEADS, SSM_HEAD_DIM, 1), lambda b: (b, 0, 0, 0))],
        out_shape=[jax.ShapeDtypeStruct((db, SSM_HEADS, SSM_HEAD_DIM, SSM_STATE), F32),
                   jax.ShapeDtypeStruct((db, SSM_HEADS, SSM_HEAD_DIM, 1), F32)],
        compiler_params=_params("arbitrary"),
        name="ssm_decode_state",
    )(state_ssm, xdt4, dec4, b4, c4)

    y = y4.reshape(db, SSM_WIDTH)
    out = pl.pallas_call(
        _ssm_decode_post_kernel,
        grid=(1,),
        in_specs=[pl.BlockSpec((db, SSM_WIDTH), zmap), pl.BlockSpec((db, SSM_WIDTH), zmap),
                  pl.BlockSpec((db, SSM_WIDTH), lambda i: (rblk, Z_OFF // SSM_WIDTH)),
                  vec(SSM_WIDTH), vec(SSM_WIDTH)],
        out_specs=pl.BlockSpec((PAD_ROWS, SSM_WIDTH), zmap),
        out_shape=jax.ShapeDtypeStruct((PAD_ROWS, SSM_WIDTH), F32),
        compiler_params=_params("arbitrary"),
        name="ssm_decode_post",
    )(y, xs_c, h, sc["d_e"], sc["g_e"])
    return out, s_new


def _lambda_init(layer):
    return 0.8 - 0.6 * math.exp(-0.3 * layer)


def _token_mixing(x, layer, batch, seq, db, w, cache_k, cache_v, state_conv, state_ssm, page_table):
    m_rows = x.shape[0]
    n_prompt = batch * seq
    lam0 = _lambda_init(layer)
    xn = _rmsnorm(x, w["mix_norm"])
    h = _matmul("in_proj", [(xn, D_MODEL, 0)], [(w["w_in"], D_MODEL, 0)], [[(0, 0)]], _epi_identity, [],
                IN_COLS_PAD, F32, tn=512, tm_cap=1024)
    qn, kn = _qk_prep(h, w["q_norm"], w["k_norm"])
    lam_vecs = [w[k].astype(F32).reshape(1, HEAD_DIM) for k in ("lambda_q1", "lambda_k1", "lambda_q2", "lambda_k2")]
    v_rows = h[:, Q_COLS + K_COLS:Q_COLS + K_COLS + V_COLS]
    xbc_rows = h[:, X_OFF:X_OFF + CONV_CH]

    att = _attn_prompt(qn, kn, h, lam_vecs, w["attn_subln"], lam0, batch, seq, m_rows)
    sc = _ssm_consts(w["conv_w"], w["conv_b"], w["dt_bias"], w["a_log"], w["d_skip"], w["ssm_norm"])
    ssm, ssm_state_p = _ssd_prompt(h, sc, batch, seq, m_rows)

    qs = qn[n_prompt:n_prompt + db].astype(F32).reshape(db, 1, Q_COLS)
    ks = kn[n_prompt:n_prompt + db].reshape(db, 1, K_COLS)
    vs = v_rows[n_prompt:n_prompt + db].reshape(db, 1, V_COLS)
    att_s = _attn_decode(qs, ks, vs, cache_k, cache_v, page_table, layer, lam_vecs, w["attn_subln"], lam0)
    att_s = jnp.pad(att_s.reshape(db, ATT_WIDTH), ((0, PAD_ROWS - db), (0, 0))).astype(BF16)
    att = lax.dynamic_update_slice(att, att_s, (n_prompt, 0))
    ssm_s, ssm_state_s = _ssm_decode(h, sc, state_conv[layer], state_ssm, layer, n_prompt, db)
    ssm = lax.dynamic_update_slice(ssm, ssm_s.astype(BF16), (n_prompt, 0))

    half = ATT_WIDTH
    x = _matmul("out_proj", [(att, half, 0), (ssm, half, 0)], [(w["w_out"], half, 0), (w["w_out"], half, 1)],
                [[(0, 0), (1, 1)]], _epi_residual, [x], D_MODEL, F32, tn=512, tm_cap=1024)

    k_p = kn[:n_prompt].reshape(batch, seq, KV_HEADS, 2 * HEAD_DIM)
    v_p = v_rows[:n_prompt].reshape(batch, seq, KV_HEADS, V_DIM)
    conv_p = xbc_rows[:n_prompt].reshape(batch, seq, CONV_CH)[:, seq - (CONV_WIDTH - 1):]
    k_s = ks.reshape(db, 1, KV_HEADS, 2 * HEAD_DIM)
    v_s = vs.reshape(db, 1, KV_HEADS, V_DIM)
    conv_s = jnp.concatenate([state_conv[layer][:, 1:].astype(F32), xbc_rows[n_prompt:n_prompt + db][:, None, :]], axis=1)
    return x, (k_p, v_p, conv_p, ssm_state_p, k_s, v_s, conv_s, ssm_state_s)


def _ple(x, g, p, w_gate, w_proj):
    xn = _rmsnorm(x, g)
    return _matmul("ple", [(xn, D_MODEL, 0), (p, PLE_DIM, 0)], [(w_gate, D_MODEL, 0), (w_proj, PLE_DIM, 0)],
                   [[(0, 0)], [(1, 1)]], _epi_gated, [x], D_MODEL, F32, tn=512, tm_cap=1024)


def kernel(x_prompt, x_sample, cache_k, cache_v, state_conv, state_ssm, page_table, p_prompt, p_sample, ffn1_norm, ffn1_w_gate, ffn1_w_up, ffn1_w_down, mix_norm, w_in, q_norm, k_norm, lambda_q1, lambda_k1, lambda_q2, lambda_k2, attn_subln, conv_w, conv_b, dt_bias, a_log, d_skip, ssm_norm, w_out, ffn2_norm, ffn2_w_gate, ffn2_w_up, ffn2_w_down, ple_norm, ple_w_gate, ple_w_proj):
    batch, seq, d = x_prompt.shape
    db = x_sample.shape[0]
    depth = w_in.shape[0]
    n_prompt = batch * seq
    x = jnp.concatenate([x_prompt.reshape(n_prompt, d), x_sample.reshape(db, d),
                         jnp.zeros((PAD_ROWS - db, d), F32)], axis=0)
    p_all = jnp.concatenate([p_prompt.reshape(depth, n_prompt, PLE_DIM), p_sample.reshape(depth, db, PLE_DIM),
                             jnp.zeros((depth, PAD_ROWS - db, PLE_DIM), F32)], axis=1).astype(BF16)
    outs = []
    for i in range(depth):
        x = _ffn(x, ffn1_norm[i], ffn1_w_gate[i].astype(BF16), ffn1_w_up[i].astype(BF16), ffn1_w_down[i].astype(BF16))
        w = dict(mix_norm=mix_norm[i],
                 w_in=jnp.pad(w_in[i].astype(BF16), ((0, 0), (0, IN_COLS_PAD - IN_COLS))),
                 q_norm=q_norm[i], k_norm=k_norm[i], lambda_q1=lambda_q1[i], lambda_k1=lambda_k1[i],
                 lambda_q2=lambda_q2[i], lambda_k2=lambda_k2[i], attn_subln=attn_subln[i], conv_w=conv_w[i],
                 conv_b=conv_b[i], dt_bias=dt_bias[i], a_log=a_log[i], d_skip=d_skip[i], ssm_norm=ssm_norm[i],
                 w_out=w_out[i].astype(BF16))
        x, o = _token_mixing(x, i, batch, seq, db, w, cache_k, cache_v, state_conv, state_ssm, page_table)
        outs.append(o)
        x = _ffn(x, ffn2_norm[i], ffn2_w_gate[i].astype(BF16), ffn2_w_up[i].astype(BF16), ffn2_w_down[i].astype(BF16))
        x = _ple(x, ple_norm[i], p_all[i], ple_w_gate[i].astype(BF16), ple_w_proj[i].astype(BF16))
    stacked = [jnp.stack([o[k] for o in outs]) for k in range(8)]
    y_prompt = x[:n_prompt].reshape(batch, seq, d)
    y_sample = x[n_prompt:n_prompt + db].reshape(db, 1, d)
    return (y_prompt, y_sample, *stacked)
```

```python
import functools
import math

import jax
import jax.numpy as jnp
from jax import lax
from jax.experimental import pallas as pl
from jax.experimental.pallas import tpu as pltpu

F32 = jnp.float32
BF16 = jnp.bfloat16

D_MODEL = 4096
PAGE_SIZE = 128
HEAD_DIM = 64
KV_HEADS = 8
KV_REP = 2
V_DIM = 128
ATT_WIDTH = KV_HEADS * KV_REP * V_DIM
Q_COLS = KV_HEADS * KV_REP * 2 * HEAD_DIM
K_COLS = KV_HEADS * 2 * HEAD_DIM
V_COLS = KV_HEADS * V_DIM
SSM_WIDTH = 2048
SSM_HEAD_DIM = 64
SSM_HEADS = 32
SSM_GROUPS = 8
SSM_HPG = 4
SSM_STATE = 128
SSM_GROUP_W = SSM_WIDTH // SSM_GROUPS
CONV_WIDTH = 4
CONV_CH = SSM_WIDTH + 2 * SSM_GROUPS * SSM_STATE
SSD_CHUNK = 128
D_FF = 11008
PLE_DIM = 256
EPS = 1e-6
NEG = -1e30
Z_OFF = Q_COLS + K_COLS + V_COLS
X_OFF = Z_OFF + SSM_WIDTH
B_OFF = X_OFF + SSM_WIDTH
C_OFF = B_OFF + SSM_GROUPS * SSM_STATE
DT_OFF = C_OFF + SSM_GROUPS * SSM_STATE
IN_COLS = DT_OFF + SSM_HEADS
IN_COLS_PAD = 10752
PAD_ROWS = 16
LANES = 128
VMEM_LIMIT_BYTES = 56 * 1024 * 1024


def _params(*sem):
    return pltpu.CompilerParams(dimension_semantics=sem, vmem_limit_bytes=VMEM_LIMIT_BYTES)


def _row_tile(m, cap):
    best = 16
    for t in range(16, cap + 1, 16):
        if m % t == 0:
            best = t
    return best


def _split3(x):
    h = x.astype(BF16)
    r = x - h.astype(F32)
    m = r.astype(BF16)
    l = (r - m.astype(F32)).astype(BF16)
    return h, m, l


def _dot(a, b):
    return jnp.dot(a, b, preferred_element_type=F32)


def _dot01_rhs(x, e):
    h, m, l = _split3(x)
    return _dot(h, e) + _dot(m, e) + _dot(l, e)


def _dot01_lhs(e, x):
    h, m, l = _split3(x)
    return _dot(e, h) + _dot(e, m) + _dot(e, l)


def _silu(x):
    return x * jax.nn.sigmoid(x)


def _softplus(x):
    return jnp.maximum(x, 0.0) + jnp.log1p(jnp.exp(-jnp.abs(x)))


def _rmsnorm_kernel(x_ref, g_ref, o_ref):
    x = x_ref[...]
    ms = jnp.mean(x * x, axis=-1, keepdims=True)
    o_ref[...] = (x * lax.rsqrt(ms + EPS) * g_ref[...]).astype(o_ref.dtype)


def _rmsnorm(x, g):
    m, d = x.shape
    tm = _row_tile(m, 320)
    return pl.pallas_call(
        _rmsnorm_kernel,
        grid=(m // tm,),
        in_specs=[pl.BlockSpec((tm, d), lambda i: (i, 0)), pl.BlockSpec((1, d), lambda i: (0, 0))],
        out_specs=pl.BlockSpec((tm, d), lambda i: (i, 0)),
        out_shape=jax.ShapeDtypeStruct((m, d), BF16),
        compiler_params=_params("arbitrary"),
        name="rmsnorm",
    )(x, g.reshape(1, d))


def _mm_kernel(*refs, nx, w_cast, groups, epi):
    nw, ncast = len(w_cast), sum(w_cast)
    xs, ws = refs[:nx], refs[nx:nx + nw]
    es, o_ref, scrs = refs[nx + nw:len(refs) - 1 - ncast], refs[len(refs) - 1 - ncast], refs[len(refs) - ncast:]
    scr_of = {}
    for wi, c in enumerate(w_cast):
        if c:
            scr_of[wi] = scrs[len(scr_of)]

    if scr_of:
        @pl.when(pl.program_id(1) == 0)
        def _():
            for wi, scr in scr_of.items():
                scr[...] = ws[wi][0].astype(BF16)

    accs = []
    for grp in groups:
        acc = None
        for xi, wi in grp:
            w = scr_of[wi][...] if wi in scr_of else ws[wi][0]
            d = _dot(xs[xi][...], w)
            acc = d if acc is None else acc + d
        accs.append(acc)
    o_ref[...] = epi(*accs, *[e[...] for e in es]).astype(o_ref.dtype)


def _matmul(name, layer, xs, ws, groups, epi, epi_args, n, out_dtype, tn, tm_cap):
    m = xs[0][0].shape[0]
    tm = _row_tile(m, tm_cap)
    in_specs = []
    for _, kb, ki in xs:
        in_specs.append(pl.BlockSpec((tm, kb), lambda j, i, ki=ki: (i, ki)))
    for _, kb, ki in ws:
        in_specs.append(pl.BlockSpec((1, kb, tn), lambda j, i, ki=ki: (layer, ki, j)))
    for _ in epi_args:
        in_specs.append(pl.BlockSpec((tm, tn), lambda j, i: (i, j)))
    w_cast = tuple(a.dtype != BF16 for a, _, _ in ws)
    kern = functools.partial(_mm_kernel, nx=len(xs), w_cast=w_cast, groups=groups, epi=epi)
    return pl.pallas_call(
        kern,
        grid=(n // tn, m // tm),
        in_specs=in_specs,
        out_specs=pl.BlockSpec((tm, tn), lambda j, i: (i, j)),
        out_shape=jax.ShapeDtypeStruct((m, n), out_dtype),
        scratch_shapes=[pltpu.VMEM((kb, tn), BF16) for (_, kb, _), c in zip(ws, w_cast) if c],
        compiler_params=_params("arbitrary", "arbitrary"),
        name=name,
    )(*[a for a, _, _ in xs], *[a for a, _, _ in ws], *epi_args)


def _epi_swiglu(a, b):
    return _silu(a) * b


def _epi_half_residual(acc, res):
    return res + 0.5 * acc


def _epi_residual(acc, res):
    return res + acc


def _epi_identity(acc):
    return acc


def _epi_gated(a, c, res):
    return res + jax.nn.sigmoid(a) * c


def _ffn(x, layer, g, wg, wu, wd):
    _, d, f = wg.shape
    xn = _rmsnorm(x, g)
    h = _matmul("ffn_gate_up", layer, [(xn, d, 0)], [(wg, d, 0), (wu, d, 0)], [[(0, 0)], [(0, 1)]], _epi_swiglu, [],
                f, BF16, tn=256, tm_cap=1024)
    return _matmul("ffn_down", layer, [(h, f, 0)], [(wd, f, 0)], [[(0, 0)]], _epi_half_residual, [x],
                   d, F32, tn=512, tm_cap=432)


def _qk_prep_kernel(q_ref, k_ref, qg_ref, kg_ref, bd_ref, qo_ref, ko_ref):
    bd = bd_ref[...]

    def norm(x, g):
        ms = _dot01_rhs(x * x, bd) * (1.0 / HEAD_DIM)
        return x * lax.rsqrt(ms + EPS) * g

    qg = qg_ref[...]
    kg = kg_ref[...]
    for c in range(Q_COLS // LANES):
        sl = slice(c * LANES, (c + 1) * LANES)
        qo_ref[:, sl] = (norm(q_ref[:, sl], qg) * (HEAD_DIM ** -0.5)).astype(qo_ref.dtype)
    for c in range(K_COLS // LANES):
        sl = slice(c * LANES, (c + 1) * LANES)
        ko_ref[:, sl] = norm(k_ref[:, sl], kg)


def _qk_prep(h, q_g, k_g):
    m = h.shape[0]
    tm = _row_tile(m, 320)
    lane = jnp.arange(LANES)
    bd = (lane[:, None] // HEAD_DIM == lane[None, :] // HEAD_DIM).astype(BF16)
    qg2 = jnp.tile(q_g.astype(F32), LANES // HEAD_DIM).reshape(1, LANES)
    kg2 = jnp.tile(k_g.astype(F32), LANES // HEAD_DIM).reshape(1, LANES)
    return pl.pallas_call(
        _qk_prep_kernel,
        grid=(m // tm,),
        in_specs=[
            pl.BlockSpec((tm, Q_COLS), lambda i: (i, 0)),
            pl.BlockSpec((tm, K_COLS), lambda i: (i, Q_COLS // K_COLS)),
            pl.BlockSpec((1, LANES), lambda i: (0, 0)),
            pl.BlockSpec((1, LANES), lambda i: (0, 0)),
            pl.BlockSpec((LANES, LANES), lambda i: (0, 0)),
        ],
        out_specs=[pl.BlockSpec((tm, Q_COLS), lambda i: (i, 0)), pl.BlockSpec((tm, K_COLS), lambda i: (i, 0))],
        out_shape=[jax.ShapeDtypeStruct((m, Q_COLS), BF16), jax.ShapeDtypeStruct((m, K_COLS), F32)],
        compiler_params=_params("arbitrary"),
        name="qk_norm",
    )(h, h, qg2, kg2, bd)


def _lambda(lq1, lk1, lq2, lk2, lam0):
    s1 = jnp.sum(lq1[...] * lk1[...], axis=-1, keepdims=True)
    s2 = jnp.sum(lq2[...] * lk2[...], axis=-1, keepdims=True)
    return jnp.exp(s1) - jnp.exp(s2) + lam0


def _sub_layer_norm(o, g, lam0):
    ms = jnp.mean(o * o, axis=-1, keepdims=True)
    return (o * lax.rsqrt(ms + EPS) * g) * (1.0 - lam0)


def _attn_prompt_kernel(lq1, lk1, lq2, lk2, sg_ref, q_ref, k_ref, v_ref, o_ref, kb_sc, vt_sc, m_sc, l_sc, acc_sc,
                        *, tq, nq, lam0):
    qi = pl.program_id(2)

    @pl.when(qi == 0)
    def _():
        kb_sc[...] = k_ref[...].astype(BF16)
        for j in range(nq):
            vt_sc[j] = v_ref[j * tq:(j + 1) * tq, :].T.astype(BF16)

    q2 = q_ref[...]
    qs = jnp.concatenate([q2[:, :LANES], q2[:, LANES:]], axis=0)
    lane = lax.broadcasted_iota(jnp.int32, qs.shape, 1)
    zero = jnp.zeros_like(qs)
    qm = (jnp.where(lane < HEAD_DIM, qs, zero), jnp.where(lane >= HEAD_DIM, qs, zero))
    m_sc[...] = jnp.full(m_sc.shape, -jnp.inf, F32)
    l_sc[...] = jnp.zeros(l_sc.shape, F32)
    acc_sc[...] = jnp.zeros(acc_sc.shape, F32)

    def block(kj, masked):
        start = pl.multiple_of(kj * tq, tq)
        kb = kb_sc[pl.ds(start, tq), :]
        vt = vt_sc[kj]
        for m in range(2):
            st = lax.dot_general(kb, qm[m], (((1,), (1,)), ((), ())), preferred_element_type=F32)
            if masked:
                key = lax.broadcasted_iota(jnp.int32, st.shape, 0)
                qry = lax.broadcasted_iota(jnp.int32, st.shape, 1) % tq
                st = jnp.where(key <= qry, st, NEG)
            m_prev = m_sc[m]
            m_new = jnp.maximum(m_prev, jnp.max(st, axis=0, keepdims=True))
            alpha = jnp.exp(m_prev - m_new)
            p = jnp.exp(st - m_new)
            l_sc[m] = alpha * l_sc[m] + jnp.sum(p, axis=0, keepdims=True)
            acc_sc[m] = alpha * acc_sc[m] + _dot(vt, p.astype(BF16))
            m_sc[m] = m_new

    def body(kj, carry):
        block(kj, False)
        return carry

    lax.fori_loop(0, qi, body, 0)
    block(qi, True)

    lam = _lambda(lq1, lk1, lq2, lk2, lam0)
    ot = acc_sc[0] / l_sc[0] - lam * (acc_sc[1] / l_sc[1])
    ms = jnp.mean(ot * ot, axis=0, keepdims=True)
    on = (ot * lax.rsqrt(ms + EPS)).T
    on = ((on * sg_ref[...]) * (1.0 - lam0)).astype(o_ref.dtype)
    o_ref[:, :LANES] = on[:tq]
    o_ref[:, LANES:] = on[tq:]


def _lam_specs():
    zeros = (0,) * 2
    return [pl.BlockSpec((1, HEAD_DIM), lambda *a: zeros) for _ in range(4)] + [pl.BlockSpec((1, V_DIM), lambda *a: zeros)]


def _attn_prompt(qn, kn, h, lam_vecs, subln_g, lam0, batch, seq, m_rows):
    tq = 256 if seq % 256 == 0 else seq
    nq = seq // tq
    kern = functools.partial(_attn_prompt_kernel, tq=tq, nq=nq, lam0=lam0)
    v_blk = (Q_COLS + K_COLS) // V_DIM
    return pl.pallas_call(
        kern,
        grid=(batch, KV_HEADS, nq),
        in_specs=_lam_specs() + [
            pl.BlockSpec((tq, 2 * LANES), lambda b, g, i: (b * nq + i, g)),
            pl.BlockSpec((seq, LANES), lambda b, g, i: (b, g)),
            pl.BlockSpec((seq, V_DIM), lambda b, g, i: (b, v_blk + g)),
        ],
        out_specs=pl.BlockSpec((tq, 2 * V_DIM), lambda b, g, i: (b * nq + i, g)),
        out_shape=jax.ShapeDtypeStruct((m_rows, ATT_WIDTH), BF16),
        scratch_shapes=[pltpu.VMEM((seq, LANES), BF16), pltpu.VMEM((nq, V_DIM, tq), BF16),
                        pltpu.VMEM((2, 1, 2 * tq), F32), pltpu.VMEM((2, 1, 2 * tq), F32),
                        pltpu.VMEM((2, V_DIM, 2 * tq), F32)],
        compiler_params=_params("arbitrary", "arbitrary", "arbitrary"),
        name="attn_prompt",
    )(*lam_vecs, subln_g.reshape(1, V_DIM), qn, kn, h)


DECODE_ROWS = 2 * KV_HEADS * KV_REP


def _attn_decode_kernel(pt_ref, lq1, lk1, lq2, lk2, sg_ref, q_ref, kn_ref, vn_ref, *refs, pages, lam0):
    ck_refs, cv_refs = refs[:pages], refs[pages:2 * pages]
    o_ref, m_sc, l_sc, acc_sc = refs[2 * pages:]
    p = pl.program_id(1)
    page_rows = PAGE_SIZE * KV_HEADS

    @pl.when(p == 0)
    def _():
        m_sc[...] = jnp.full(m_sc.shape, -jnp.inf, F32)
        l_sc[...] = jnp.zeros(l_sc.shape, F32)
        acc_sc[...] = jnp.zeros(acc_sc.shape, F32)

    q = q_ref[0]
    qb = q.astype(BF16)
    row_head = (lax.broadcasted_iota(jnp.int32, (DECODE_ROWS, page_rows), 0) % (KV_HEADS * KV_REP)) // KV_REP
    col_head = lax.broadcasted_iota(jnp.int32, (DECODE_ROWS, page_rows), 1) % KV_HEADS
    same_head = row_head == col_head

    scores, values = [], []
    for j in range(pages):
        kb = ck_refs[j][0, 0].astype(BF16)
        s = lax.dot_general(qb, kb, (((1,), (1,)), ((), ())), preferred_element_type=F32)
        scores.append(jnp.where(same_head, s, NEG))
        values.append(cv_refs[j][0, 0].astype(BF16))
    m_prev = m_sc[...]
    m_new = m_prev
    for s in scores:
        m_new = jnp.maximum(m_new, jnp.max(s, axis=-1, keepdims=True))
    alpha = jnp.exp(m_prev - m_new)
    l_new = alpha * l_sc[...]
    acc = alpha * acc_sc[...]
    for s, vb in zip(scores, values):
        pr = jnp.exp(s - m_new)
        l_new = l_new + jnp.sum(pr, axis=-1, keepdims=True)
        acc = acc + _dot(pr.astype(BF16), vb)
    m_sc[...] = m_new
    l_sc[...] = l_new
    acc_sc[...] = acc

    @pl.when(p == pl.num_programs(1) - 1)
    def _():
        s = jnp.sum(q * kn_ref[0], axis=-1, keepdims=True)
        m_fin = jnp.maximum(m_new, s)
        a = jnp.exp(m_new - m_fin)
        pr = jnp.exp(s - m_fin)
        l_fin = a * l_new + pr
        o = (a * acc + pr * vn_ref[0]) / l_fin
        half = DECODE_ROWS // 2
        lam = _lambda(lq1, lk1, lq2, lk2, lam0)
        o_ref[0] = _sub_layer_norm(o[:half] - lam * o[half:], sg_ref[...], lam0)


def _attn_decode(qs, ks, vs, cache_k, cache_v, page_table, layer, lam_vecs, subln_g, lam0):
    db, n_pages = page_table.shape
    depth, n_phys = cache_k.shape[:2]
    pages = 4 if n_pages % 4 == 0 else (2 if n_pages % 2 == 0 else 1)
    page_rows = PAGE_SIZE * KV_HEADS
    ck = cache_k.reshape(depth, n_phys, page_rows, 2 * HEAD_DIM)
    cv = cache_v.reshape(depth, n_phys, page_rows, V_DIM)
    q5 = qs.reshape(db, KV_HEADS, KV_REP, 2, HEAD_DIM)
    z = jnp.zeros((db, KV_HEADS, KV_REP, HEAD_DIM), F32)
    qall = jnp.stack([jnp.concatenate([q5[:, :, :, 0], z], -1), jnp.concatenate([z, q5[:, :, :, 1]], -1)], axis=1)
    qall = qall.reshape(db, DECODE_ROWS, LANES)
    per_row = lambda t: jnp.broadcast_to(t.reshape(db, 1, KV_HEADS, 1, LANES),
                                         (db, 2, KV_HEADS, KV_REP, LANES)).reshape(db, DECODE_ROWS, LANES)
    kern = functools.partial(_attn_decode_kernel, pages=pages, lam0=lam0)
    row = lambda b, p, pt: (b, 0, 0)
    page = lambda j: (lambda b, p, pt: (layer, pt[b, p * pages + j], 0, 0))
    grid_spec = pltpu.PrefetchScalarGridSpec(
        num_scalar_prefetch=1,
        grid=(db, n_pages // pages),
        in_specs=_lam_specs() + [pl.BlockSpec((1, DECODE_ROWS, LANES), row) for _ in range(3)]
        + [pl.BlockSpec((1, 1, page_rows, 2 * HEAD_DIM), page(j)) for j in range(pages)]
        + [pl.BlockSpec((1, 1, page_rows, V_DIM), page(j)) for j in range(pages)],
        out_specs=pl.BlockSpec((1, DECODE_ROWS // 2, V_DIM), row),
        scratch_shapes=[pltpu.VMEM((DECODE_ROWS, 1), F32), pltpu.VMEM((DECODE_ROWS, 1), F32),
                        pltpu.VMEM((DECODE_ROWS, V_DIM), F32)],
    )
    out = pl.pallas_call(
        kern,
        grid_spec=grid_spec,
        out_shape=jax.ShapeDtypeStruct((db, DECODE_ROWS // 2, V_DIM), F32),
        compiler_params=_params("arbitrary", "arbitrary"),
        name="attn_decode",
    )(page_table, *lam_vecs, subln_g.reshape(1, V_DIM), qall, per_row(ks), per_row(vs),
      *([ck] * pages), *([cv] * pages))
    return out.reshape(db, ATT_WIDTH)


def _gate_norm(y, xs_c, z, d_e, g_e):
    y = (y + d_e * xs_c) * _silu(z)
    outs = []
    for g in range(SSM_GROUPS):
        yg = y[:, g * SSM_GROUP_W:(g + 1) * SSM_GROUP_W]
        ms = jnp.mean(yg * yg, axis=-1, keepdims=True)
        outs.append(yg * lax.rsqrt(ms + EPS) * g_e[:, g * SSM_GROUP_W:(g + 1) * SSM_GROUP_W])
    return outs


def _ssd_prompt_kernel(z_ref, x_ref, b_ref, c_ref, xn_ref, wdt_ref, cwx, cwb, cwc, cbx, cbb, cbc, dtb_e, alog_e, dtb_c, alog_c,
                       d_e, g_e, tril_ref, exp_ref, o_ref, s_ref, xpx, xpb, xpc, st_sc, y_sc):
    c = pl.program_id(1)
    q = SSD_CHUNK

    @pl.when(c == 0)
    def _():
        xpx[0:8, :] = jnp.zeros((8, xpx.shape[1]), F32)
        xpb[0:8, :] = jnp.zeros((8, xpb.shape[1]), F32)
        xpc[0:8, :] = jnp.zeros((8, xpc.shape[1]), F32)
        st_sc[...] = jnp.zeros(st_sc.shape, F32)

    def conv(src, pad, w, b):
        pad[8:8 + q, :] = src[...]
        y = b[...]
        for t in range(CONV_WIDTH):
            y = y + w[t:t + 1, :] * pad[5 + t:5 + t + q, :]
        tail = pad[q:q + 8, :]
        pad[0:8, :] = tail
        return _silu(y)

    xs_c = conv(x_ref, xpx, cwx, cbx)
    bc = conv(b_ref, xpb, cwb, cbb)
    cc = conv(c_ref, xpc, cwc, cbc)

    tril = tril_ref[...]
    dtr = _dot(xn_ref[...], wdt_ref[...])
    dt_c = _softplus(dtr + dtb_c[...])
    acs_c = _dot01_lhs(tril, dt_c * (-jnp.exp(alog_c[...])))
    acs_t = acs_c.T
    dt_e = _softplus(_dot01_rhs(dtr, exp_ref[...]) + dtb_e[...])
    acs_e = _dot01_lhs(tril, dt_e * (-jnp.exp(alog_e[...])))
    a_last = acs_e[q - 1:q, :]
    xdt = xs_c * dt_e
    xdtd = (xdt * jnp.exp(a_last - acs_e)).astype(BF16)
    xdt = xdt.astype(BF16)
    ea = jnp.exp(acs_e)
    dec_end = jnp.exp(a_last)

    ii = lax.broadcasted_iota(jnp.int32, (q, q), 0)
    jj = lax.broadcasted_iota(jnp.int32, (q, q), 1)
    causal = jj <= ii
    lane_lo = jj < SSM_HEAD_DIM

    for g in range(SSM_GROUPS):
        gs = slice(g * SSM_GROUP_W, (g + 1) * SSM_GROUP_W)
        bg = bc[:, g * SSM_STATE:(g + 1) * SSM_STATE].astype(BF16)
        cg = cc[:, g * SSM_STATE:(g + 1) * SSM_STATE].astype(BF16)
        cb = lax.dot_general(cg, bg, (((1,), (1,)), ((), ())), preferred_element_type=F32)
        st = st_sc[g]
        y_off = _dot(cg, st.astype(BF16)) * ea[:, gs]
        pairs = []
        for pr in range(SSM_HPG // 2):
            col = g * SSM_GROUP_W + pr * LANES
            xpair = xdt[:, col:col + LANES]
            ys = []
            for hh in range(2):
                head = g * SSM_HPG + pr * 2 + hh
                seg = acs_c[:, head:head + 1] - acs_t[head:head + 1, :]
                lmat = jnp.where(causal, jnp.exp(seg), 0.0)
                ys.append(_dot((cb * lmat).astype(BF16), xpair))
            pairs.append(jnp.where(lane_lo, ys[0], ys[1]))
        y_sc[:, gs] = jnp.concatenate(pairs, axis=1) + y_off
        upd = lax.dot_general(bg, xdtd[:, gs], (((0,), (0,)), ((), ())), preferred_element_type=F32)
        st_sc[g] = st * dec_end[:, gs] + upd

    outs = _gate_norm(y_sc[...], xs_c, z_ref[...], d_e[...], g_e[...])
    for g in range(SSM_GROUPS):
        o_ref[:, g * SSM_GROUP_W:(g + 1) * SSM_GROUP_W] = outs[g].astype(o_ref.dtype)

    @pl.when(c == pl.num_programs(1) - 1)
    def _():
        for g in range(SSM_GROUPS):
            t = st_sc[g].T
            for r in range(SSM_HPG):
                s_ref[0, g * SSM_HPG + r] = t[r * SSM_HEAD_DIM:(r + 1) * SSM_HEAD_DIM, :]


def _ssm_consts(conv_w, conv_b, dt_bias, a_log, d_skip, ssm_g):
    rep = lambda v: jnp.repeat(v.astype(F32), SSM_HEAD_DIM).reshape(1, SSM_WIDTH)
    pad = lambda v: jnp.pad(v.astype(F32), (0, LANES - SSM_HEADS)).reshape(1, LANES)
    head_of_col = jnp.arange(SSM_WIDTH) // SSM_HEAD_DIM
    expand = (jnp.arange(LANES)[:, None] == head_of_col[None, :]).astype(BF16)
    tril = (jnp.arange(SSD_CHUNK)[None, :] <= jnp.arange(SSD_CHUNK)[:, None]).astype(BF16)
    return dict(conv_w=conv_w.astype(F32), conv_b=conv_b.astype(F32).reshape(1, CONV_CH),
                dtb_e=rep(dt_bias), alog_e=rep(a_log), dtb_c=pad(dt_bias), alog_c=pad(a_log),
                d_e=rep(d_skip), g_e=ssm_g.astype(F32).reshape(1, SSM_WIDTH), tril=tril, expand=expand)


def _conv_param_specs(zmap):
    n_x, n_b = SSM_WIDTH, SSM_GROUPS * SSM_STATE
    cmap = lambda k: (lambda *a: (0, k))
    return [pl.BlockSpec((CONV_WIDTH, n_x), cmap(0)), pl.BlockSpec((CONV_WIDTH, n_b), cmap(2)),
            pl.BlockSpec((CONV_WIDTH, n_b), cmap(3)),
            pl.BlockSpec((1, n_x), cmap(0)), pl.BlockSpec((1, n_b), cmap(2)), pl.BlockSpec((1, n_b), cmap(3))]


def _ssd_prompt(h, xn, w_dt, sc, batch, seq, m_rows):
    q = SSD_CHUNK
    nc = seq // q
    rb = lambda b, c: b * nc + c
    n_b = SSM_GROUPS * SSM_STATE
    zmap = lambda b, c: (0, 0)
    vec = lambda n: pl.BlockSpec((1, n), zmap)
    in_specs = [
        pl.BlockSpec((q, SSM_WIDTH), lambda b, c: (rb(b, c), Z_OFF // SSM_WIDTH)),
        pl.BlockSpec((q, SSM_WIDTH), lambda b, c: (rb(b, c), X_OFF // SSM_WIDTH)),
        pl.BlockSpec((q, n_b), lambda b, c: (rb(b, c), B_OFF // n_b)),
        pl.BlockSpec((q, n_b), lambda b, c: (rb(b, c), C_OFF // n_b)),
        pl.BlockSpec((q, D_MODEL), lambda b, c: (rb(b, c), 0)),
        pl.BlockSpec((D_MODEL, LANES), zmap),
    ] + _conv_param_specs(zmap) + [
        vec(SSM_WIDTH), vec(SSM_WIDTH), vec(LANES), vec(LANES), vec(SSM_WIDTH), vec(SSM_WIDTH),
        pl.BlockSpec((q, q), zmap), pl.BlockSpec((LANES, SSM_WIDTH), zmap),
    ]
    out, state = pl.pallas_call(
        _ssd_prompt_kernel,
        grid=(batch, nc),
        in_specs=in_specs,
        out_specs=[pl.BlockSpec((q, SSM_WIDTH), lambda b, c: (rb(b, c), 0)),
                   pl.BlockSpec((1, SSM_HEADS, SSM_HEAD_DIM, SSM_STATE), lambda b, c: (b, 0, 0, 0))],
        out_shape=[jax.ShapeDtypeStruct((m_rows, SSM_WIDTH), BF16),
                   jax.ShapeDtypeStruct((batch, SSM_HEADS, SSM_HEAD_DIM, SSM_STATE), F32)],
        scratch_shapes=[pltpu.VMEM((q + 8, SSM_WIDTH), F32), pltpu.VMEM((q + 8, n_b), F32),
                        pltpu.VMEM((q + 8, n_b), F32),
                        pltpu.VMEM((SSM_GROUPS, SSM_STATE, SSM_GROUP_W), F32), pltpu.VMEM((q, SSM_WIDTH), F32)],
        compiler_params=_params("arbitrary", "arbitrary"),
        name="ssd_prompt",
    )(h, h, h, h, xn, w_dt, sc["conv_w"], sc["conv_w"], sc["conv_w"], sc["conv_b"], sc["conv_b"], sc["conv_b"],
      sc["dtb_e"], sc["alog_e"], sc["dtb_c"], sc["alog_c"], sc["d_e"], sc["g_e"], sc["tril"], sc["expand"])
    return out, state


def _ssm_decode_pre_kernel(x_ref, b_ref, c_ref, xn_ref, wdt_ref, prev_ref, cwx, cwb, cwc, cbx, cbb, cbc, dtb_e, dtb_c, alog_c,
                           exp_ref, xs_o, b_o, c_o, xdt_o, dec_o):
    def conv(src, col, w, b):
        n = src.shape[1]
        y = b[...] + w[CONV_WIDTH - 1:CONV_WIDTH, :] * src[...]
        for t in range(CONV_WIDTH - 1):
            y = y + w[t:t + 1, :] * prev_ref[t, :, col:col + n]
        return _silu(y)

    xs_c = conv(x_ref, 0, cwx, cbx)
    xs_o[...] = xs_c
    b_o[...] = conv(b_ref, SSM_WIDTH, cwb, cbb)
    c_o[...] = conv(c_ref, SSM_WIDTH + SSM_GROUPS * SSM_STATE, cwc, cbc)
    dtr = _dot(xn_ref[...], wdt_ref[...])[:x_ref.shape[0]]
    dt_c = _softplus(dtr + dtb_c[...])
    dec_o[...] = jnp.exp(dt_c * (-jnp.exp(alog_c[...])))
    dt_e = _softplus(_dot01_rhs(dtr, exp_ref[...]) + dtb_e[...])
    xdt_o[...] = xs_c * dt_e


def _ssm_decode_state_kernel(s_ref, xdt_ref, dec_ref, b_ref, c_ref, so_ref, y_ref):
    for head in range(SSM_HEADS):
        g = head // SSM_HPG
        s = dec_ref[0, head] * s_ref[0, 0, head] + xdt_ref[0, head] * b_ref[0, g]
        so_ref[0, head] = s
        y_ref[0, head] = jnp.sum(s * c_ref[0, g], axis=-1, keepdims=True)


def _ssm_decode_post_kernel(y_ref, xs_ref, z_ref, d_e, g_e, o_ref):
    outs = _gate_norm(y_ref[...], xs_ref[...], z_ref[...], d_e[...], g_e[...])
    o_ref[...] = jnp.zeros(o_ref.shape, o_ref.dtype)
    db = y_ref.shape[0]
    for g in range(SSM_GROUPS):
        o_ref[0:db, g * SSM_GROUP_W:(g + 1) * SSM_GROUP_W] = outs[g]


def _ssm_decode(h, xn, w_dt, sc, state_conv_l, state_ssm, layer, row0, db):
    n_b = SSM_GROUPS * SSM_STATE
    rblk = row0 // db
    zmap = lambda i: (0, 0)
    vec = lambda n: pl.BlockSpec((1, n), zmap)
    prev = jnp.transpose(state_conv_l, (1, 0, 2)).astype(F32)
    f = lambda n: jax.ShapeDtypeStruct((db, n), F32)
    xs_c, b_c, c_c, xdt, dec = pl.pallas_call(
        _ssm_decode_pre_kernel,
        grid=(1,),
        in_specs=[
            pl.BlockSpec((db, SSM_WIDTH), lambda i: (rblk, X_OFF // SSM_WIDTH)),
            pl.BlockSpec((db, n_b), lambda i: (rblk, B_OFF // n_b)),
            pl.BlockSpec((db, n_b), lambda i: (rblk, C_OFF // n_b)),
            pl.BlockSpec((PAD_ROWS, D_MODEL), lambda i: (row0 // PAD_ROWS, 0)),
            pl.BlockSpec((D_MODEL, LANES), zmap),
            pl.BlockSpec((CONV_WIDTH - 1, db, CONV_CH), lambda i: (0, 0, 0)),
        ] + _conv_param_specs(zmap) + [vec(SSM_WIDTH), vec(LANES), vec(LANES),
                                       pl.BlockSpec((LANES, SSM_WIDTH), zmap)],
        out_specs=[pl.BlockSpec((db, SSM_WIDTH), zmap), pl.BlockSpec((db, n_b), zmap), pl.BlockSpec((db, n_b), zmap),
                   pl.BlockSpec((db, SSM_WIDTH), zmap), pl.BlockSpec((db, LANES), zmap)],
        out_shape=[f(SSM_WIDTH), f(n_b), f(n_b), f(SSM_WIDTH), f(LANES)],
        compiler_params=_params("arbitrary"),
        name="ssm_decode_pre",
    )(h, h, h, xn, w_dt, prev, sc["conv_w"], sc["conv_w"], sc["conv_w"], sc["conv_b"], sc["conv_b"], sc["conv_b"],
      sc["dtb_e"], sc["dtb_c"], sc["alog_c"], sc["expand"])

    xdt4 = xdt.reshape(db, SSM_HEADS, SSM_HEAD_DIM, 1)
    dec4 = dec[:, :SSM_HEADS].reshape(db, SSM_HEADS, 1, 1)
    b4 = b_c.reshape(db, SSM_GROUPS, 1, SSM_STATE)
    c4 = c_c.reshape(db, SSM_GROUPS, 1, SSM_STATE)
    s_new, y4 = pl.pallas_call(
        _ssm_decode_state_kernel,
        grid=(db,),
        in_specs=[
            pl.BlockSpec((1, 1, SSM_HEADS, SSM_HEAD_DIM, SSM_STATE), lambda b: (layer, b, 0, 0, 0)),
            pl.BlockSpec((1, SSM_HEADS, SSM_HEAD_DIM, 1), lambda b: (b, 0, 0, 0)),
            pl.BlockSpec((1, SSM_HEADS, 1, 1), lambda b: (b, 0, 0, 0)),
            pl.BlockSpec((1, SSM_GROUPS, 1, SSM_STATE), lambda b: (b, 0, 0, 0)),
            pl.BlockSpec((1, SSM_GROUPS, 1, SSM_STATE), lambda b: (b, 0, 0, 0)),
        ],
        out_specs=[pl.BlockSpec((1, SSM_HEADS, SSM_HEAD_DIM, SSM_STATE), lambda b: (b, 0, 0, 0)),
                   pl.BlockSpec((1, SSM_HEADS, SSM_HEAD_DIM, 1), lambda b: (b, 0, 0, 0))],
        out_shape=[jax.ShapeDtypeStruct((db, SSM_HEADS, SSM_HEAD_DIM, SSM_STATE), F32),
                   jax.ShapeDtypeStruct((db, SSM_HEADS, SSM_HEAD_DIM, 1), F32)],
        compiler_params=_params("arbitrary"),
        name="ssm_decode_state",
    )(state_ssm, xdt4, dec4, b4, c4)

    y = y4.reshape(db, SSM_WIDTH)
    out = pl.pallas_call(
        _ssm_decode_post_kernel,
        grid=(1,),
        in_specs=[pl.BlockSpec((db, SSM_WIDTH), zmap), pl.BlockSpec((db, SSM_WIDTH), zmap),
                  pl.BlockSpec((db, SSM_WIDTH), lambda i: (rblk, Z_OFF // SSM_WIDTH)),
                  vec(SSM_WIDTH), vec(SSM_WIDTH)],
        out_specs=pl.BlockSpec((PAD_ROWS, SSM_WIDTH), zmap),
        out_shape=jax.ShapeDtypeStruct((PAD_ROWS, SSM_WIDTH), F32),
        compiler_params=_params("arbitrary"),
        name="ssm_decode_post",
    )(y, xs_c, h, sc["d_e"], sc["g_e"])
    return out, s_new


def _lambda_init(layer):
    return 0.8 - 0.6 * math.exp(-0.3 * layer)


def _token_mixing(x, layer, batch, seq, db, w, cache_k, cache_v, state_conv, state_ssm, page_table):
    m_rows = x.shape[0]
    n_prompt = batch * seq
    lam0 = _lambda_init(layer)
    xn = _rmsnorm(x, w["mix_norm"])
    h = _matmul("in_proj", layer, [(xn, D_MODEL, 0)], [(w["w_in"], D_MODEL, 0)], [[(0, 0)]], _epi_identity, [],
                DT_OFF, F32, tn=512, tm_cap=1024)
    w_dt = jnp.pad(w["w_in"][layer, :, DT_OFF:].astype(BF16), ((0, 0), (0, LANES - SSM_HEADS)))
    qn, kn = _qk_prep(h, w["q_norm"], w["k_norm"])
    lam_vecs = [w[k].astype(F32).reshape(1, HEAD_DIM) for k in ("lambda_q1", "lambda_k1", "lambda_q2", "lambda_k2")]
    dec = slice(n_prompt, n_prompt + db)
    v_off = Q_COLS + K_COLS

    att = _attn_prompt(qn, kn, h, lam_vecs, w["attn_subln"], lam0, batch, seq, m_rows)
    sc = _ssm_consts(w["conv_w"], w["conv_b"], w["dt_bias"], w["a_log"], w["d_skip"], w["ssm_norm"])
    ssm, ssm_state_p = _ssd_prompt(h, xn, w_dt, sc, batch, seq, m_rows)

    qs = qn[dec].astype(F32)
    ks = kn[dec]
    vs = h[dec, v_off:v_off + V_COLS]
    att_s = _attn_decode(qs, ks, vs, cache_k, cache_v, page_table, layer, lam_vecs, w["attn_subln"], lam0)
    att_s = jnp.pad(att_s, ((0, PAD_ROWS - db), (0, 0))).astype(BF16)
    att = lax.dynamic_update_slice(att, att_s, (n_prompt, 0))
    ssm_s, ssm_state_s = _ssm_decode(h, xn, w_dt, sc, state_conv[layer], state_ssm, layer, n_prompt, db)
    ssm = lax.dynamic_update_slice(ssm, ssm_s.astype(BF16), (n_prompt, 0))

    half = ATT_WIDTH
    x = _matmul("out_proj", layer, [(att, half, 0), (ssm, half, 0)], [(w["w_out"], half, 0), (w["w_out"], half, 1)],
                [[(0, 0), (1, 1)]], _epi_residual, [x], D_MODEL, F32, tn=512, tm_cap=1024)

    k_p = kn[:n_prompt].reshape(batch, seq, KV_HEADS, 2 * HEAD_DIM)
    v_p = h[:n_prompt, v_off:v_off + V_COLS].reshape(batch, seq, KV_HEADS, V_DIM)
    conv_p = jnp.stack([h[(b + 1) * seq - (CONV_WIDTH - 1):(b + 1) * seq, X_OFF:X_OFF + CONV_CH] for b in range(batch)])
    k_s = ks.reshape(db, 1, KV_HEADS, 2 * HEAD_DIM)
    v_s = vs.reshape(db, 1, KV_HEADS, V_DIM)
    conv_s = jnp.concatenate([state_conv[layer][:, 1:].astype(F32), h[dec, X_OFF:X_OFF + CONV_CH][:, None, :]], axis=1)
    return x, (k_p, v_p, conv_p, ssm_state_p, k_s, v_s, conv_s, ssm_state_s)


def _ple(x, layer, g, p, w_gate, w_proj):
    xn = _rmsnorm(x, g)
    return _matmul("ple", layer, [(xn, D_MODEL, 0), (p, PLE_DIM, 0)], [(w_gate, D_MODEL, 0), (w_proj, PLE_DIM, 0)],
                   [[(0, 0)], [(1, 1)]], _epi_gated, [x], D_MODEL, F32, tn=512, tm_cap=1024)


def kernel(x_prompt, x_sample, cache_k, cache_v, state_conv, state_ssm, page_table, p_prompt, p_sample, ffn1_norm, ffn1_w_gate, ffn1_w_up, ffn1_w_down, mix_norm, w_in, q_norm, k_norm, lambda_q1, lambda_k1, lambda_q2, lambda_k2, attn_subln, conv_w, conv_b, dt_bias, a_log, d_skip, ssm_norm, w_out, ffn2_norm, ffn2_w_gate, ffn2_w_up, ffn2_w_down, ple_norm, ple_w_gate, ple_w_proj):
    batch, seq, d = x_prompt.shape
    db = x_sample.shape[0]
    depth = w_in.shape[0]
    n_prompt = batch * seq
    x = jnp.concatenate([x_prompt.reshape(n_prompt, d), x_sample.reshape(db, d),
                         jnp.zeros((PAD_ROWS - db, d), F32)], axis=0)
    p_all = jnp.concatenate([p_prompt.reshape(depth, n_prompt, PLE_DIM), p_sample.reshape(depth, db, PLE_DIM),
                             jnp.zeros((depth, PAD_ROWS - db, PLE_DIM), F32)], axis=1).astype(BF16)
    wd1 = ffn1_w_down.astype(BF16)
    wd2 = ffn2_w_down.astype(BF16)
    outs = []
    for i in range(depth):
        x = _ffn(x, i, ffn1_norm[i], ffn1_w_gate, ffn1_w_up, wd1)
        w = dict(mix_norm=mix_norm[i], w_in=w_in,
                 q_norm=q_norm[i], k_norm=k_norm[i], lambda_q1=lambda_q1[i], lambda_k1=lambda_k1[i],
                 lambda_q2=lambda_q2[i], lambda_k2=lambda_k2[i], attn_subln=attn_subln[i], conv_w=conv_w[i],
                 conv_b=conv_b[i], dt_bias=dt_bias[i], a_log=a_log[i], d_skip=d_skip[i], ssm_norm=ssm_norm[i],
                 w_out=w_out)
        x, o = _token_mixing(x, i, batch, seq, db, w, cache_k, cache_v, state_conv, state_ssm, page_table)
        outs.append(o)
        x = _ffn(x, i, ffn2_norm[i], ffn2_w_gate, ffn2_w_up, wd2)
        x = _ple(x, i, ple_norm[i], p_all[i], ple_w_gate, ple_w_proj)
    stacked = [jnp.stack([o[k] for o in outs]) for k in range(8)]
    y_prompt = x[:n_prompt].reshape(batch, seq, d)
    y_sample = x[n_prompt:n_prompt + db].reshape(db, 1, d)
    return (y_prompt, y_sample, *stacked)
```

```python
import functools
import math

import jax
import jax.numpy as jnp
from jax import lax
from jax.experimental import pallas as pl
from jax.experimental.pallas import tpu as pltpu

F32 = jnp.float32
BF16 = jnp.bfloat16

D_MODEL = 4096
PAGE_SIZE = 128
HEAD_DIM = 64
KV_HEADS = 8
KV_REP = 2
V_DIM = 128
ATT_WIDTH = KV_HEADS * KV_REP * V_DIM
Q_COLS = KV_HEADS * KV_REP * 2 * HEAD_DIM
K_COLS = KV_HEADS * 2 * HEAD_DIM
V_COLS = KV_HEADS * V_DIM
SSM_WIDTH = 2048
SSM_HEAD_DIM = 64
SSM_HEADS = 32
SSM_GROUPS = 8
SSM_HPG = 4
SSM_STATE = 128
SSM_GROUP_W = SSM_WIDTH // SSM_GROUPS
CONV_WIDTH = 4
CONV_CH = SSM_WIDTH + 2 * SSM_GROUPS * SSM_STATE
SSD_CHUNK = 128
D_FF = 11008
PLE_DIM = 256
EPS = 1e-6
NEG = -1e30
Z_OFF = Q_COLS + K_COLS + V_COLS
X_OFF = Z_OFF + SSM_WIDTH
B_OFF = X_OFF + SSM_WIDTH
C_OFF = B_OFF + SSM_GROUPS * SSM_STATE
DT_OFF = C_OFF + SSM_GROUPS * SSM_STATE
IN_COLS = DT_OFF + SSM_HEADS
IN_COLS_PAD = 10752
PAD_ROWS = 16
LANES = 128
VMEM_LIMIT_BYTES = 56 * 1024 * 1024


def _params(*sem):
    return pltpu.CompilerParams(dimension_semantics=sem, vmem_limit_bytes=VMEM_LIMIT_BYTES)


def _row_tile(m, cap):
    best = 16
    for t in range(16, cap + 1, 16):
        if m % t == 0:
            best = t
    return best


def _split3(x):
    h = x.astype(BF16)
    r = x - h.astype(F32)
    m = r.astype(BF16)
    l = (r - m.astype(F32)).astype(BF16)
    return h, m, l


def _dot(a, b):
    return jnp.dot(a, b, preferred_element_type=F32)


def _dot01_rhs(x, e):
    h, m, l = _split3(x)
    return _dot(h, e) + _dot(m, e) + _dot(l, e)


def _dot01_lhs(e, x):
    h, m, l = _split3(x)
    return _dot(e, h) + _dot(e, m) + _dot(e, l)


def _silu(x):
    return x * jax.nn.sigmoid(x)


def _softplus(x):
    return jnp.maximum(x, 0.0) + jnp.log1p(jnp.exp(-jnp.abs(x)))


def _rmsnorm_kernel(x_ref, g_ref, o_ref):
    x = x_ref[...]
    ms = jnp.mean(x * x, axis=-1, keepdims=True)
    o_ref[...] = (x * lax.rsqrt(ms + EPS) * g_ref[...]).astype(o_ref.dtype)


def _rmsnorm(x, g):
    m, d = x.shape
    tm = _row_tile(m, 320)
    return pl.pallas_call(
        _rmsnorm_kernel,
        grid=(m // tm,),
        in_specs=[pl.BlockSpec((tm, d), lambda i: (i, 0)), pl.BlockSpec((1, d), lambda i: (0, 0))],
        out_specs=pl.BlockSpec((tm, d), lambda i: (i, 0)),
        out_shape=jax.ShapeDtypeStruct((m, d), BF16),
        compiler_params=_params("arbitrary"),
        name="rmsnorm",
    )(x, g.reshape(1, d))


def _mm_kernel(*refs, nx, w_cast, w_trans, groups, epi):
    nw, ncast = len(w_cast), sum(w_cast)
    xs, ws = refs[:nx], refs[nx:nx + nw]
    es, o_ref, scrs = refs[nx + nw:len(refs) - 1 - ncast], refs[len(refs) - 1 - ncast], refs[len(refs) - ncast:]
    scr_of = {}
    for wi, c in enumerate(w_cast):
        if c:
            scr_of[wi] = scrs[len(scr_of)]

    if scr_of:
        @pl.when(pl.program_id(1) == 0)
        def _():
            for wi, scr in scr_of.items():
                scr[...] = ws[wi][0].astype(BF16)

    accs = []
    for grp in groups:
        acc = None
        for xi, wi in grp:
            w = scr_of[wi][...] if wi in scr_of else ws[wi][0]
            if w_trans[wi]:
                d = lax.dot_general(xs[xi][...], w, (((1,), (1,)), ((), ())), preferred_element_type=F32)
            else:
                d = _dot(xs[xi][...], w)
            acc = d if acc is None else acc + d
        accs.append(acc)
    o_ref[...] = epi(*accs, *[e[...] for e in es]).astype(o_ref.dtype)


def _matmul(name, layer, xs, ws, groups, epi, epi_args, n, out_dtype, tn, tm_cap, w_trans=None):
    m = xs[0][0].shape[0]
    tm = _row_tile(m, tm_cap)
    w_trans = w_trans or (False,) * len(ws)
    in_specs = []
    for _, kb, ki in xs:
        in_specs.append(pl.BlockSpec((tm, kb), lambda j, i, ki=ki: (i, ki)))
    w_blocks = []
    for (_, kb, ki), t in zip(ws, w_trans):
        if t:
            w_blocks.append((tn, kb))
            in_specs.append(pl.BlockSpec((1, tn, kb), lambda j, i, ki=ki: (layer, j, ki)))
        else:
            w_blocks.append((kb, tn))
            in_specs.append(pl.BlockSpec((1, kb, tn), lambda j, i, ki=ki: (layer, ki, j)))
    for _ in epi_args:
        in_specs.append(pl.BlockSpec((tm, tn), lambda j, i: (i, j)))
    w_cast = tuple(a.dtype != BF16 for a, _, _ in ws)
    kern = functools.partial(_mm_kernel, nx=len(xs), w_cast=w_cast, w_trans=w_trans, groups=groups, epi=epi)
    return pl.pallas_call(
        kern,
        grid=(n // tn, m // tm),
        in_specs=in_specs,
        out_specs=pl.BlockSpec((tm, tn), lambda j, i: (i, j)),
        out_shape=jax.ShapeDtypeStruct((m, n), out_dtype),
        scratch_shapes=[pltpu.VMEM(blk, BF16) for blk, c in zip(w_blocks, w_cast) if c],
        compiler_params=_params("arbitrary", "arbitrary"),
        name=name,
    )(*[a for a, _, _ in xs], *[a for a, _, _ in ws], *epi_args)


def _epi_swiglu(a, b):
    return _silu(a) * b


def _epi_half_residual(acc, res):
    return res + 0.5 * acc


def _epi_residual(acc, res):
    return res + acc


def _epi_identity(acc):
    return acc


def _epi_gated(a, c, res):
    return res + jax.nn.sigmoid(a) * c


def _ffn(x, layer, g, wg, wu, wd):
    _, d, f = wg.shape
    xn = _rmsnorm(x, g)
    h = _matmul("ffn_gate_up", layer, [(xn, d, 0)], [(wg, d, 0), (wu, d, 0)], [[(0, 0)], [(0, 1)]], _epi_swiglu, [],
                f, BF16, tn=256, tm_cap=1024)
    return _matmul("ffn_down", layer, [(h, f, 0)], [(wd, f, 0)], [[(0, 0)]], _epi_half_residual, [x],
                   d, F32, tn=512, tm_cap=432)


Q_SCALE = HEAD_DIM ** -0.5 * math.log2(math.e)


def _qk_prep_kernel(q_ref, k_ref, qg_ref, kg_ref, bd_ref, qo_ref, ko_ref):
    bd = bd_ref[...]

    def norm(x, g):
        ms = _dot01_rhs(x * x, bd) * (1.0 / HEAD_DIM)
        return x * lax.rsqrt(ms + EPS) * g

    qg = qg_ref[...]
    kg = kg_ref[...]
    for c in range(Q_COLS // LANES):
        sl = slice(c * LANES, (c + 1) * LANES)
        qo_ref[:, sl] = (norm(q_ref[:, sl], qg) * Q_SCALE).astype(qo_ref.dtype)
    for c in range(K_COLS // LANES):
        sl = slice(c * LANES, (c + 1) * LANES)
        ko_ref[:, sl] = norm(k_ref[:, sl], kg)


def _qk_prep(h, q_g, k_g):
    m = h.shape[0]
    tm = _row_tile(m, 320)
    lane = jnp.arange(LANES)
    bd = (lane[:, None] // HEAD_DIM == lane[None, :] // HEAD_DIM).astype(BF16)
    qg2 = jnp.tile(q_g.astype(F32), LANES // HEAD_DIM).reshape(1, LANES)
    kg2 = jnp.tile(k_g.astype(F32), LANES // HEAD_DIM).reshape(1, LANES)
    return pl.pallas_call(
        _qk_prep_kernel,
        grid=(m // tm,),
        in_specs=[
            pl.BlockSpec((tm, Q_COLS), lambda i: (i, 0)),
            pl.BlockSpec((tm, K_COLS), lambda i: (i, Q_COLS // K_COLS)),
            pl.BlockSpec((1, LANES), lambda i: (0, 0)),
            pl.BlockSpec((1, LANES), lambda i: (0, 0)),
            pl.BlockSpec((LANES, LANES), lambda i: (0, 0)),
        ],
        out_specs=[pl.BlockSpec((tm, Q_COLS), lambda i: (i, 0)), pl.BlockSpec((tm, K_COLS), lambda i: (i, 0))],
        out_shape=[jax.ShapeDtypeStruct((m, Q_COLS), BF16), jax.ShapeDtypeStruct((m, K_COLS), F32)],
        compiler_params=_params("arbitrary"),
        name="qk_norm",
    )(h, h, qg2, kg2, bd)


def _lambda(lq1, lk1, lq2, lk2, lam0):
    s1 = jnp.sum(lq1[...] * lk1[...], axis=-1, keepdims=True)
    s2 = jnp.sum(lq2[...] * lk2[...], axis=-1, keepdims=True)
    return jnp.exp(s1) - jnp.exp(s2) + lam0


def _sub_layer_norm(o, g, lam0):
    ms = jnp.mean(o * o, axis=-1, keepdims=True)
    return (o * lax.rsqrt(ms + EPS) * g) * (1.0 - lam0)


SUM_ROWS = 16


def _attn_prompt_kernel(lq1, lk1, lq2, lk2, sg_ref, q_ref, k_ref, v_ref, o_ref, kb_sc, vt_sc, m_sc, acc_sc,
                        *, tq, nq, lam0):
    qi = pl.program_id(2)

    @pl.when(qi == 0)
    def _():
        kb_sc[...] = k_ref[...].astype(BF16)
        ones_rows = (lax.broadcasted_iota(jnp.int32, (SUM_ROWS, tq), 0) == 0).astype(BF16)
        for j in range(nq):
            vt_sc[j, :V_DIM, :] = v_ref[j * tq:(j + 1) * tq, :].T.astype(BF16)
            vt_sc[j, V_DIM:, :] = ones_rows

    q2 = q_ref[...]
    qs = jnp.concatenate([q2[:, :LANES], q2[:, LANES:]], axis=0)
    lane = lax.broadcasted_iota(jnp.int32, qs.shape, 1)
    zero = jnp.zeros_like(qs)
    qm = (jnp.where(lane < HEAD_DIM, qs, zero), jnp.where(lane >= HEAD_DIM, qs, zero))
    m_sc[...] = jnp.full(m_sc.shape, -jnp.inf, F32)
    acc_sc[...] = jnp.zeros(acc_sc.shape, F32)

    def block(kj, masked):
        start = pl.multiple_of(kj * tq, tq)
        kb = kb_sc[pl.ds(start, tq), :]
        vt = vt_sc[kj]
        sts = [lax.dot_general(kb, qm[m], (((1,), (1,)), ((), ())), preferred_element_type=F32) for m in range(2)]
        for m in range(2):
            st = sts[m]
            if masked:
                key = lax.broadcasted_iota(jnp.int32, st.shape, 0)
                qry = lax.broadcasted_iota(jnp.int32, st.shape, 1) % tq
                st = jnp.where(key <= qry, st, NEG)
            m_prev = m_sc[m]
            m_new = jnp.maximum(m_prev, jnp.max(st, axis=0, keepdims=True))
            p = jnp.exp2(st - m_new).astype(BF16)
            acc_sc[m] = jnp.exp2(m_prev - m_new) * acc_sc[m] + _dot(vt, p)
            m_sc[m] = m_new

    def body(kj, carry):
        block(kj, False)
        return carry

    lax.fori_loop(0, qi, body, 0)
    block(qi, True)

    lam = _lambda(lq1, lk1, lq2, lk2, lam0)
    o0, o1 = acc_sc[0], acc_sc[1]
    ot = o0[:V_DIM] / o0[V_DIM:V_DIM + 1] - lam * (o1[:V_DIM] / o1[V_DIM:V_DIM + 1])
    ms = jnp.mean(ot * ot, axis=0, keepdims=True)
    on = (ot * lax.rsqrt(ms + EPS)).T
    on = ((on * sg_ref[...]) * (1.0 - lam0)).astype(o_ref.dtype)
    o_ref[:, :LANES] = on[:tq]
    o_ref[:, LANES:] = on[tq:]


def _lam_specs():
    zeros = (0,) * 2
    return [pl.BlockSpec((1, HEAD_DIM), lambda *a: zeros) for _ in range(4)] + [pl.BlockSpec((1, V_DIM), lambda *a: zeros)]


def _attn_prompt(qn, kn, h, lam_vecs, subln_g, lam0, batch, seq, m_rows):
    tq = next((t for t in (512, 256) if seq % t == 0), seq)
    nq = seq // tq
    kern = functools.partial(_attn_prompt_kernel, tq=tq, nq=nq, lam0=lam0)
    v_blk = (Q_COLS + K_COLS) // V_DIM
    return pl.pallas_call(
        kern,
        grid=(batch, KV_HEADS, nq),
        in_specs=_lam_specs() + [
            pl.BlockSpec((tq, 2 * LANES), lambda b, g, i: (b * nq + i, g)),
            pl.BlockSpec((seq, LANES), lambda b, g, i: (b, g)),
            pl.BlockSpec((seq, V_DIM), lambda b, g, i: (b, v_blk + g)),
        ],
        out_specs=pl.BlockSpec((tq, 2 * V_DIM), lambda b, g, i: (b * nq + i, g)),
        out_shape=jax.ShapeDtypeStruct((m_rows, ATT_WIDTH), BF16),
        scratch_shapes=[pltpu.VMEM((seq, LANES), BF16), pltpu.VMEM((nq, V_DIM + SUM_ROWS, tq), BF16),
                        pltpu.VMEM((2, 1, 2 * tq), F32), pltpu.VMEM((2, V_DIM + SUM_ROWS, 2 * tq), F32)],
        compiler_params=_params("arbitrary", "arbitrary", "arbitrary"),
        name="attn_prompt",
    )(*lam_vecs, subln_g.reshape(1, V_DIM), qn, kn, h)


DECODE_ROWS = 2 * KV_HEADS * KV_REP


def _attn_decode_kernel(pt_ref, lq1, lk1, lq2, lk2, sg_ref, q_ref, kn_ref, vn_ref, *refs, pages, lam0):
    ck_refs, cv_refs = refs[:pages], refs[pages:2 * pages]
    o_ref, m_sc, l_sc, acc_sc = refs[2 * pages:]
    p = pl.program_id(1)
    page_rows = PAGE_SIZE * KV_HEADS

    @pl.when(p == 0)
    def _():
        m_sc[...] = jnp.full(m_sc.shape, -jnp.inf, F32)
        l_sc[...] = jnp.zeros(l_sc.shape, F32)
        acc_sc[...] = jnp.zeros(acc_sc.shape, F32)

    q = q_ref[0]
    qb = q.astype(BF16)
    row_head = (lax.broadcasted_iota(jnp.int32, (DECODE_ROWS, page_rows), 0) % (KV_HEADS * KV_REP)) // KV_REP
    col_head = lax.broadcasted_iota(jnp.int32, (DECODE_ROWS, page_rows), 1) % KV_HEADS
    same_head = row_head == col_head

    scores, values = [], []
    for j in range(pages):
        kb = ck_refs[j][0, 0].astype(BF16)
        s = lax.dot_general(qb, kb, (((1,), (1,)), ((), ())), preferred_element_type=F32)
        scores.append(jnp.where(same_head, s, NEG))
        values.append(cv_refs[j][0, 0].astype(BF16))
    m_prev = m_sc[...]
    m_new = m_prev
    for s in scores:
        m_new = jnp.maximum(m_new, jnp.max(s, axis=-1, keepdims=True))
    alpha = jnp.exp2(m_prev - m_new)
    l_new = alpha * l_sc[...]
    acc = alpha * acc_sc[...]
    for s, vb in zip(scores, values):
        pr = jnp.exp2(s - m_new)
        l_new = l_new + jnp.sum(pr, axis=-1, keepdims=True)
        acc = acc + _dot(pr.astype(BF16), vb)
    m_sc[...] = m_new
    l_sc[...] = l_new
    acc_sc[...] = acc

    @pl.when(p == pl.num_programs(1) - 1)
    def _():
        s = jnp.sum(q * kn_ref[0], axis=-1, keepdims=True)
        m_fin = jnp.maximum(m_new, s)
        a = jnp.exp2(m_new - m_fin)
        pr = jnp.exp2(s - m_fin)
        l_fin = a * l_new + pr
        o = (a * acc + pr * vn_ref[0]) / l_fin
        half = DECODE_ROWS // 2
        lam = _lambda(lq1, lk1, lq2, lk2, lam0)
        o_ref[0] = _sub_layer_norm(o[:half] - lam * o[half:], sg_ref[...], lam0)


def _attn_decode(qs, ks, vs, cache_k, cache_v, page_table, layer, lam_vecs, subln_g, lam0):
    db, n_pages = page_table.shape
    depth, n_phys = cache_k.shape[:2]
    pages = 4 if n_pages % 4 == 0 else (2 if n_pages % 2 == 0 else 1)
    page_rows = PAGE_SIZE * KV_HEADS
    ck = cache_k.reshape(depth, n_phys, page_rows, 2 * HEAD_DIM)
    cv = cache_v.reshape(depth, n_phys, page_rows, V_DIM)
    q5 = qs.reshape(db, KV_HEADS, KV_REP, 2, HEAD_DIM)
    z = jnp.zeros((db, KV_HEADS, KV_REP, HEAD_DIM), F32)
    qall = jnp.stack([jnp.concatenate([q5[:, :, :, 0], z], -1), jnp.concatenate([z, q5[:, :, :, 1]], -1)], axis=1)
    qall = qall.reshape(db, DECODE_ROWS, LANES)
    per_row = lambda t: jnp.broadcast_to(t.reshape(db, 1, KV_HEADS, 1, LANES),
                                         (db, 2, KV_HEADS, KV_REP, LANES)).reshape(db, DECODE_ROWS, LANES)
    kern = functools.partial(_attn_decode_kernel, pages=pages, lam0=lam0)
    row = lambda b, p, pt: (b, 0, 0)
    page = lambda j: (lambda b, p, pt: (layer, pt[b, p * pages + j], 0, 0))
    grid_spec = pltpu.PrefetchScalarGridSpec(
        num_scalar_prefetch=1,
        grid=(db, n_pages // pages),
        in_specs=_lam_specs() + [pl.BlockSpec((1, DECODE_ROWS, LANES), row) for _ in range(3)]
        + [pl.BlockSpec((1, 1, page_rows, 2 * HEAD_DIM), page(j)) for j in range(pages)]
        + [pl.BlockSpec((1, 1, page_rows, V_DIM), page(j)) for j in range(pages)],
        out_specs=pl.BlockSpec((1, DECODE_ROWS // 2, V_DIM), row),
        scratch_shapes=[pltpu.VMEM((DECODE_ROWS, 1), F32), pltpu.VMEM((DECODE_ROWS, 1), F32),
                        pltpu.VMEM((DECODE_ROWS, V_DIM), F32)],
    )
    out = pl.pallas_call(
        kern,
        grid_spec=grid_spec,
        out_shape=jax.ShapeDtypeStruct((db, DECODE_ROWS // 2, V_DIM), F32),
        compiler_params=_params("arbitrary", "arbitrary"),
        name="attn_decode",
    )(page_table, *lam_vecs, subln_g.reshape(1, V_DIM), qall, per_row(ks), per_row(vs),
      *([ck] * pages), *([cv] * pages))
    return out.reshape(db, ATT_WIDTH)


def _gate_norm(y, xs_c, z, d_e, g_e):
    y = (y + d_e * xs_c) * _silu(z)
    outs = []
    for g in range(SSM_GROUPS):
        yg = y[:, g * SSM_GROUP_W:(g + 1) * SSM_GROUP_W]
        ms = jnp.mean(yg * yg, axis=-1, keepdims=True)
        outs.append(yg * lax.rsqrt(ms + EPS) * g_e[:, g * SSM_GROUP_W:(g + 1) * SSM_GROUP_W])
    return outs


def _dt_raw(xn, wdt_ref):
    w = wdt_ref[0].astype(BF16)
    w = jnp.concatenate([w, jnp.zeros((LANES - SSM_HEADS, w.shape[1]), BF16)], axis=0)
    return lax.dot_general(xn, w, (((1,), (1,)), ((), ())), preferred_element_type=F32)


def _dt_weight_spec(layer):
    return pl.BlockSpec((1, SSM_HEADS, D_MODEL), lambda *a: (layer, DT_OFF // SSM_HEADS, 0))


def _ssd_prompt_kernel(z_ref, x_ref, b_ref, c_ref, xn_ref, wdt_ref, cwx, cwb, cwc, cbx, cbb, cbc, dtb_c, alog_c,
                       d_e, g_e, tril_ref, exp_ref, o_ref, s_ref, xpx, xpb, xpc, st_sc, y_sc):
    c = pl.program_id(1)
    q = SSD_CHUNK

    @pl.when(c == 0)
    def _():
        xpx[0:8, :] = jnp.zeros((8, xpx.shape[1]), F32)
        xpb[0:8, :] = jnp.zeros((8, xpb.shape[1]), F32)
        xpc[0:8, :] = jnp.zeros((8, xpc.shape[1]), F32)
        st_sc[...] = jnp.zeros(st_sc.shape, F32)

    def conv(src, pad, w, b):
        pad[8:8 + q, :] = src[...]
        y = b[...]
        for t in range(CONV_WIDTH):
            y = y + w[t:t + 1, :] * pad[5 + t:5 + t + q, :]
        tail = pad[q:q + 8, :]
        pad[0:8, :] = tail
        return _silu(y)

    xs_c = conv(x_ref, xpx, cwx, cbx)
    bc = conv(b_ref, xpb, cwb, cbb)
    cc = conv(c_ref, xpc, cwc, cbc)

    tril = tril_ref[...]
    dtr = _dt_raw(xn_ref[...], wdt_ref)
    dt_c = _softplus(dtr + dtb_c[...])
    acs_c = _dot01_lhs(tril, dt_c * (-jnp.exp(alog_c[...])))
    acs_t = acs_c.T
    dt_e = _dot01_rhs(dt_c, exp_ref[...])
    acs_e = _dot01_rhs(acs_c, exp_ref[...])
    a_last = acs_e[q - 1:q, :]
    xdt = xs_c * dt_e
    xdtd = (xdt * jnp.exp(a_last - acs_e)).astype(BF16)
    xdt = xdt.astype(BF16)
    ea = jnp.exp(acs_e)
    dec_end = jnp.exp(a_last)

    ii = lax.broadcasted_iota(jnp.int32, (q, q), 0)
    jj = lax.broadcasted_iota(jnp.int32, (q, q), 1)
    causal = jj <= ii
    lane_lo = jj < SSM_HEAD_DIM

    for g in range(SSM_GROUPS):
        gs = slice(g * SSM_GROUP_W, (g + 1) * SSM_GROUP_W)
        bg = bc[:, g * SSM_STATE:(g + 1) * SSM_STATE].astype(BF16)
        cg = cc[:, g * SSM_STATE:(g + 1) * SSM_STATE].astype(BF16)
        cb = lax.dot_general(cg, bg, (((1,), (1,)), ((), ())), preferred_element_type=F32)
        st = st_sc[g]
        y_off = _dot(cg, st.astype(BF16)) * ea[:, gs]
        pairs = []
        for pr in range(SSM_HPG // 2):
            col = g * SSM_GROUP_W + pr * LANES
            xpair = xdt[:, col:col + LANES]
            ys = []
            for hh in range(2):
                head = g * SSM_HPG + pr * 2 + hh
                seg = acs_c[:, head:head + 1] - acs_t[head:head + 1, :]
                lmat = jnp.where(causal, jnp.exp(seg), 0.0)
                ys.append(_dot((cb * lmat).astype(BF16), xpair))
            pairs.append(jnp.where(lane_lo, ys[0], ys[1]))
        y_sc[:, gs] = jnp.concatenate(pairs, axis=1) + y_off
        upd = lax.dot_general(bg, xdtd[:, gs], (((0,), (0,)), ((), ())), preferred_element_type=F32)
        st_sc[g] = st * dec_end[:, gs] + upd

    outs = _gate_norm(y_sc[...], xs_c, z_ref[...], d_e[...], g_e[...])
    for g in range(SSM_GROUPS):
        o_ref[:, g * SSM_GROUP_W:(g + 1) * SSM_GROUP_W] = outs[g].astype(o_ref.dtype)

    @pl.when(c == pl.num_programs(1) - 1)
    def _():
        for g in range(SSM_GROUPS):
            t = st_sc[g].T
            for r in range(SSM_HPG):
                s_ref[0, g * SSM_HPG + r] = t[r * SSM_HEAD_DIM:(r + 1) * SSM_HEAD_DIM, :]


def _ssm_consts(conv_w, conv_b, dt_bias, a_log, d_skip, ssm_g):
    rep = lambda v: jnp.repeat(v.astype(F32), SSM_HEAD_DIM).reshape(1, SSM_WIDTH)
    pad = lambda v: jnp.pad(v.astype(F32), (0, LANES - SSM_HEADS)).reshape(1, LANES)
    head_of_col = jnp.arange(SSM_WIDTH) // SSM_HEAD_DIM
    expand = (jnp.arange(LANES)[:, None] == head_of_col[None, :]).astype(BF16)
    tril = (jnp.arange(SSD_CHUNK)[None, :] <= jnp.arange(SSD_CHUNK)[:, None]).astype(BF16)
    return dict(conv_w=conv_w.astype(F32), conv_b=conv_b.astype(F32).reshape(1, CONV_CH),
                dtb_c=pad(dt_bias), alog_c=pad(a_log),
                d_e=rep(d_skip), g_e=ssm_g.astype(F32).reshape(1, SSM_WIDTH), tril=tril, expand=expand)


def _conv_param_specs(zmap):
    n_x, n_b = SSM_WIDTH, SSM_GROUPS * SSM_STATE
    cmap = lambda k: (lambda *a: (0, k))
    return [pl.BlockSpec((CONV_WIDTH, n_x), cmap(0)), pl.BlockSpec((CONV_WIDTH, n_b), cmap(2)),
            pl.BlockSpec((CONV_WIDTH, n_b), cmap(3)),
            pl.BlockSpec((1, n_x), cmap(0)), pl.BlockSpec((1, n_b), cmap(2)), pl.BlockSpec((1, n_b), cmap(3))]


def _ssd_prompt(h, xn, w_dt, layer, sc, batch, seq, m_rows):
    q = SSD_CHUNK
    nc = seq // q
    rb = lambda b, c: b * nc + c
    n_b = SSM_GROUPS * SSM_STATE
    zmap = lambda b, c: (0, 0)
    vec = lambda n: pl.BlockSpec((1, n), zmap)
    in_specs = [
        pl.BlockSpec((q, SSM_WIDTH), lambda b, c: (rb(b, c), Z_OFF // SSM_WIDTH)),
        pl.BlockSpec((q, SSM_WIDTH), lambda b, c: (rb(b, c), X_OFF // SSM_WIDTH)),
        pl.BlockSpec((q, n_b), lambda b, c: (rb(b, c), B_OFF // n_b)),
        pl.BlockSpec((q, n_b), lambda b, c: (rb(b, c), C_OFF // n_b)),
        pl.BlockSpec((q, D_MODEL), lambda b, c: (rb(b, c), 0)),
        _dt_weight_spec(layer),
    ] + _conv_param_specs(zmap) + [
        vec(LANES), vec(LANES), vec(SSM_WIDTH), vec(SSM_WIDTH),
        pl.BlockSpec((q, q), zmap), pl.BlockSpec((LANES, SSM_WIDTH), zmap),
    ]
    out, state = pl.pallas_call(
        _ssd_prompt_kernel,
        grid=(batch, nc),
        in_specs=in_specs,
        out_specs=[pl.BlockSpec((q, SSM_WIDTH), lambda b, c: (rb(b, c), 0)),
                   pl.BlockSpec((1, SSM_HEADS, SSM_HEAD_DIM, SSM_STATE), lambda b, c: (b, 0, 0, 0))],
        out_shape=[jax.ShapeDtypeStruct((m_rows, SSM_WIDTH), BF16),
                   jax.ShapeDtypeStruct((batch, SSM_HEADS, SSM_HEAD_DIM, SSM_STATE), F32)],
        scratch_shapes=[pltpu.VMEM((q + 8, SSM_WIDTH), F32), pltpu.VMEM((q + 8, n_b), F32),
                        pltpu.VMEM((q + 8, n_b), F32),
                        pltpu.VMEM((SSM_GROUPS, SSM_STATE, SSM_GROUP_W), F32), pltpu.VMEM((q, SSM_WIDTH), F32)],
        compiler_params=_params("arbitrary", "arbitrary"),
        name="ssd_prompt",
    )(h, h, h, h, xn, w_dt, sc["conv_w"], sc["conv_w"], sc["conv_w"], sc["conv_b"], sc["conv_b"], sc["conv_b"],
      sc["dtb_c"], sc["alog_c"], sc["d_e"], sc["g_e"], sc["tril"], sc["expand"])
    return out, state


def _ssm_decode_pre_kernel(x_ref, b_ref, c_ref, xn_ref, wdt_ref, prev_ref, cwx, cwb, cwc, cbx, cbb, cbc, dtb_c, alog_c,
                           exp_ref, xs_o, b_o, c_o, xdt_o, dec_o):
    def conv(src, col, w, b):
        n = src.shape[1]
        y = b[...] + w[CONV_WIDTH - 1:CONV_WIDTH, :] * src[...]
        for t in range(CONV_WIDTH - 1):
            y = y + w[t:t + 1, :] * prev_ref[t, :, col:col + n]
        return _silu(y)

    xs_c = conv(x_ref, 0, cwx, cbx)
    xs_o[...] = xs_c
    b_o[...] = conv(b_ref, SSM_WIDTH, cwb, cbb)
    c_o[...] = conv(c_ref, SSM_WIDTH + SSM_GROUPS * SSM_STATE, cwc, cbc)
    dtr = _dt_raw(xn_ref[...], wdt_ref)[:x_ref.shape[0]]
    dt_c = _softplus(dtr + dtb_c[...])
    dec_o[...] = jnp.exp(dt_c * (-jnp.exp(alog_c[...])))
    xdt_o[...] = xs_c * _dot01_rhs(dt_c, exp_ref[...])


def _ssm_decode_state_kernel(s_ref, xdt_ref, dec_ref, b_ref, c_ref, so_ref, y_ref):
    for head in range(SSM_HEADS):
        g = head // SSM_HPG
        s = dec_ref[0, head] * s_ref[0, 0, head] + xdt_ref[0, head] * b_ref[0, g]
        so_ref[0, head] = s
        y_ref[0, head] = jnp.sum(s * c_ref[0, g], axis=-1, keepdims=True)


def _ssm_decode_post_kernel(y_ref, xs_ref, z_ref, d_e, g_e, o_ref):
    outs = _gate_norm(y_ref[...], xs_ref[...], z_ref[...], d_e[...], g_e[...])
    o_ref[...] = jnp.zeros(o_ref.shape, o_ref.dtype)
    db = y_ref.shape[0]
    for g in range(SSM_GROUPS):
        o_ref[0:db, g * SSM_GROUP_W:(g + 1) * SSM_GROUP_W] = outs[g]


def _ssm_decode(h, xn, w_dt, sc, state_conv_l, state_ssm, layer, row0, db):
    n_b = SSM_GROUPS * SSM_STATE
    rblk = row0 // db
    zmap = lambda i: (0, 0)
    vec = lambda n: pl.BlockSpec((1, n), zmap)
    prev = jnp.transpose(state_conv_l, (1, 0, 2)).astype(F32)
    f = lambda n: jax.ShapeDtypeStruct((db, n), F32)
    xs_c, b_c, c_c, xdt, dec = pl.pallas_call(
        _ssm_decode_pre_kernel,
        grid=(1,),
        in_specs=[
            pl.BlockSpec((db, SSM_WIDTH), lambda i: (rblk, X_OFF // SSM_WIDTH)),
            pl.BlockSpec((db, n_b), lambda i: (rblk, B_OFF // n_b)),
            pl.BlockSpec((db, n_b), lambda i: (rblk, C_OFF // n_b)),
            pl.BlockSpec((PAD_ROWS, D_MODEL), lambda i: (row0 // PAD_ROWS, 0)),
            _dt_weight_spec(layer),
            pl.BlockSpec((CONV_WIDTH - 1, db, CONV_CH), lambda i: (0, 0, 0)),
        ] + _conv_param_specs(zmap) + [vec(LANES), vec(LANES),
                                       pl.BlockSpec((LANES, SSM_WIDTH), zmap)],
        out_specs=[pl.BlockSpec((db, SSM_WIDTH), zmap), pl.BlockSpec((db, n_b), zmap), pl.BlockSpec((db, n_b), zmap),
                   pl.BlockSpec((db, SSM_WIDTH), zmap), pl.BlockSpec((db, LANES), zmap)],
        out_shape=[f(SSM_WIDTH), f(n_b), f(n_b), f(SSM_WIDTH), f(LANES)],
        compiler_params=_params("arbitrary"),
        name="ssm_decode_pre",
    )(h, h, h, xn, w_dt, prev, sc["conv_w"], sc["conv_w"], sc["conv_w"], sc["conv_b"], sc["conv_b"], sc["conv_b"],
      sc["dtb_c"], sc["alog_c"], sc["expand"])

    xdt4 = xdt.reshape(db, SSM_HEADS, SSM_HEAD_DIM, 1)
    dec4 = dec[:, :SSM_HEADS].reshape(db, SSM_HEADS, 1, 1)
    b4 = b_c.reshape(db, SSM_GROUPS, 1, SSM_STATE)
    c4 = c_c.reshape(db, SSM_GROUPS, 1, SSM_STATE)
    s_new, y4 = pl.pallas_call(
        _ssm_decode_state_kernel,
        grid=(db,),
        in_specs=[
            pl.BlockSpec((1, 1, SSM_HEADS, SSM_HEAD_DIM, SSM_STATE), lambda b: (layer, b, 0, 0, 0)),
            pl.BlockSpec((1, SSM_HEADS, SSM_HEAD_DIM, 1), lambda b: (b, 0, 0, 0)),
            pl.BlockSpec((1, SSM_HEADS, 1, 1), lambda b: (b, 0, 0, 0)),
            pl.BlockSpec((1, SSM_GROUPS, 1, SSM_STATE), lambda b: (b, 0, 0, 0)),
            pl.BlockSpec((1, SSM_GROUPS, 1, SSM_STATE), lambda b: (b, 0, 0, 0)),
        ],
        out_specs=[pl.BlockSpec((1, SSM_HEADS, SSM_HEAD_DIM, SSM_STATE), lambda b: (b, 0, 0, 0)),
                   pl.BlockSpec((1, SSM_HEADS, SSM_HEAD_DIM, 1), lambda b: (b, 0, 0, 0))],
        out_shape=[jax.ShapeDtypeStruct((db, SSM_HEADS, SSM_HEAD_DIM, SSM_STATE), F32),
                   jax.ShapeDtypeStruct((db, SSM_HEADS, SSM_HEAD_DIM, 1), F32)],
        compiler_params=_params("arbitrary"),
        name="ssm_decode_state",
    )(state_ssm, xdt4, dec4, b4, c4)

    y = y4.reshape(db, SSM_WIDTH)
    out = pl.pallas_call(
        _ssm_decode_post_kernel,
        grid=(1,),
        in_specs=[pl.BlockSpec((db, SSM_WIDTH), zmap), pl.BlockSpec((db, SSM_WIDTH), zmap),
                  pl.BlockSpec((db, SSM_WIDTH), lambda i: (rblk, Z_OFF // SSM_WIDTH)),
                  vec(SSM_WIDTH), vec(SSM_WIDTH)],
        out_specs=pl.BlockSpec((PAD_ROWS, SSM_WIDTH), zmap),
        out_shape=jax.ShapeDtypeStruct((PAD_ROWS, SSM_WIDTH), F32),
        compiler_params=_params("arbitrary"),
        name="ssm_decode_post",
    )(y, xs_c, h, sc["d_e"], sc["g_e"])
    return out, s_new


def _lambda_init(layer):
    return 0.8 - 0.6 * math.exp(-0.3 * layer)


def _token_mixing(x, layer, batch, seq, db, w, cache_k, cache_v, state_conv, state_ssm, page_table):
    m_rows = x.shape[0]
    n_prompt = batch * seq
    lam0 = _lambda_init(layer)
    xn = _rmsnorm(x, w["mix_norm"])
    w_dt = w["w_in_t"]
    h = _matmul("in_proj", layer, [(xn, D_MODEL, 0)], [(w_dt, D_MODEL, 0)], [[(0, 0)]], _epi_identity, [],
                DT_OFF, F32, tn=512, tm_cap=1024, w_trans=(True,))
    qn, kn = _qk_prep(h, w["q_norm"], w["k_norm"])
    lam_vecs = [w[k].astype(F32).reshape(1, HEAD_DIM) for k in ("lambda_q1", "lambda_k1", "lambda_q2", "lambda_k2")]
    dec = slice(n_prompt, n_prompt + db)
    v_off = Q_COLS + K_COLS

    att = _attn_prompt(qn, kn, h, lam_vecs, w["attn_subln"], lam0, batch, seq, m_rows)
    sc = _ssm_consts(w["conv_w"], w["conv_b"], w["dt_bias"], w["a_log"], w["d_skip"], w["ssm_norm"])
    ssm, ssm_state_p = _ssd_prompt(h, xn, w_dt, layer, sc, batch, seq, m_rows)

    qs = qn[dec].astype(F32)
    ks = kn[dec]
    vs = h[dec, v_off:v_off + V_COLS]
    att_s = _attn_decode(qs, ks, vs, cache_k, cache_v, page_table, layer, lam_vecs, w["attn_subln"], lam0)
    att_s = jnp.pad(att_s, ((0, PAD_ROWS - db), (0, 0))).astype(BF16)
    att = lax.dynamic_update_slice(att, att_s, (n_prompt, 0))
    ssm_s, ssm_state_s = _ssm_decode(h, xn, w_dt, sc, state_conv[layer], state_ssm, layer, n_prompt, db)
    ssm = lax.dynamic_update_slice(ssm, ssm_s.astype(BF16), (n_prompt, 0))

    half = ATT_WIDTH
    x = _matmul("out_proj", layer, [(att, half, 0), (ssm, half, 0)], [(w["w_out"], half, 0), (w["w_out"], half, 1)],
                [[(0, 0), (1, 1)]], _epi_residual, [x], D_MODEL, F32, tn=512, tm_cap=1024)

    k_p = kn[:n_prompt].reshape(batch, seq, KV_HEADS, 2 * HEAD_DIM)
    v_p = h[:n_prompt, v_off:v_off + V_COLS].reshape(batch, seq, KV_HEADS, V_DIM)
    conv_p = jnp.stack([h[(b + 1) * seq - (CONV_WIDTH - 1):(b + 1) * seq, X_OFF:X_OFF + CONV_CH] for b in range(batch)])
    k_s = ks.reshape(db, 1, KV_HEADS, 2 * HEAD_DIM)
    v_s = vs.reshape(db, 1, KV_HEADS, V_DIM)
    conv_s = jnp.concatenate([state_conv[layer][:, 1:].astype(F32), h[dec, X_OFF:X_OFF + CONV_CH][:, None, :]], axis=1)
    return x, (k_p, v_p, conv_p, ssm_state_p, k_s, v_s, conv_s, ssm_state_s)


def _ple(x, layer, g, p, w_gate, w_proj):
    xn = _rmsnorm(x, g)
    return _matmul("ple", layer, [(xn, D_MODEL, 0), (p, PLE_DIM, 0)], [(w_gate, D_MODEL, 0), (w_proj, PLE_DIM, 0)],
                   [[(0, 0)], [(1, 1)]], _epi_gated, [x], D_MODEL, F32, tn=512, tm_cap=1024)


def kernel(x_prompt, x_sample, cache_k, cache_v, state_conv, state_ssm, page_table, p_prompt, p_sample, ffn1_norm, ffn1_w_gate, ffn1_w_up, ffn1_w_down, mix_norm, w_in, q_norm, k_norm, lambda_q1, lambda_k1, lambda_q2, lambda_k2, attn_subln, conv_w, conv_b, dt_bias, a_log, d_skip, ssm_norm, w_out, ffn2_norm, ffn2_w_gate, ffn2_w_up, ffn2_w_down, ple_norm, ple_w_gate, ple_w_proj):
    batch, seq, d = x_prompt.shape
    db = x_sample.shape[0]
    depth = w_in.shape[0]
    n_prompt = batch * seq
    x = jnp.concatenate([x_prompt.reshape(n_prompt, d), x_sample.reshape(db, d),
                         jnp.zeros((PAD_ROWS - db, d), F32)], axis=0)
    p_all = jnp.concatenate([p_prompt.reshape(depth, n_prompt, PLE_DIM), p_sample.reshape(depth, db, PLE_DIM),
                             jnp.zeros((depth, PAD_ROWS - db, PLE_DIM), F32)], axis=1).astype(BF16)
    wd1 = ffn1_w_down.astype(BF16)
    wd2 = ffn2_w_down.astype(BF16)
    w_in_t = jnp.swapaxes(w_in, 1, 2)
    outs = []
    for i in range(depth):
        x = _ffn(x, i, ffn1_norm[i], ffn1_w_gate, ffn1_w_up, wd1)
        w = dict(mix_norm=mix_norm[i], w_in_t=w_in_t,
                 q_norm=q_norm[i], k_norm=k_norm[i], lambda_q1=lambda_q1[i], lambda_k1=lambda_k1[i],
                 lambda_q2=lambda_q2[i], lambda_k2=lambda_k2[i], attn_subln=attn_subln[i], conv_w=conv_w[i],
                 conv_b=conv_b[i], dt_bias=dt_bias[i], a_log=a_log[i], d_skip=d_skip[i], ssm_norm=ssm_norm[i],
                 w_out=w_out)
        x, o = _token_mixing(x, i, batch, seq, db, w, cache_k, cache_v, state_conv, state_ssm, page_table)
        outs.append(o)
        x = _ffn(x, i, ffn2_norm[i], ffn2_w_gate, ffn2_w_up, wd2)
        x = _ple(x, i, ple_norm[i], p_all[i], ple_w_gate, ple_w_proj)
    stacked = [jnp.stack([o[k] for o in outs]) for k in range(8)]
    y_prompt = x[:n_prompt].reshape(batch, seq, d)
    y_sample = x[n_prompt:n_prompt + db].reshape(db, 1, d)
    return (y_prompt, y_sample, *stacked)
```

```python
import functools
import math

import jax
import jax.numpy as jnp
from jax import lax
from jax.experimental import pallas as pl
from jax.experimental.pallas import tpu as pltpu

F32 = jnp.float32
BF16 = jnp.bfloat16

D_MODEL = 4096
PAGE_SIZE = 128
HEAD_DIM = 64
KV_HEADS = 8
KV_REP = 2
V_DIM = 128
ATT_WIDTH = KV_HEADS * KV_REP * V_DIM
Q_COLS = KV_HEADS * KV_REP * 2 * HEAD_DIM
K_COLS = KV_HEADS * 2 * HEAD_DIM
V_COLS = KV_HEADS * V_DIM
SSM_WIDTH = 2048
SSM_HEAD_DIM = 64
SSM_HEADS = 32
SSM_GROUPS = 8
SSM_HPG = 4
SSM_STATE = 128
SSM_GROUP_W = SSM_WIDTH // SSM_GROUPS
CONV_WIDTH = 4
CONV_CH = SSM_WIDTH + 2 * SSM_GROUPS * SSM_STATE
SSD_CHUNK = 128
D_FF = 11008
PLE_DIM = 256
EPS = 1e-6
NEG = -1e30
Z_OFF = Q_COLS + K_COLS + V_COLS
X_OFF = Z_OFF + SSM_WIDTH
B_OFF = X_OFF + SSM_WIDTH
C_OFF = B_OFF + SSM_GROUPS * SSM_STATE
DT_OFF = C_OFF + SSM_GROUPS * SSM_STATE
IN_COLS = DT_OFF + SSM_HEADS
PAD_ROWS = 16
LANES = 128
VMEM_LIMIT_BYTES = 56 * 1024 * 1024


def _params(*sem):
    return pltpu.CompilerParams(dimension_semantics=sem, vmem_limit_bytes=VMEM_LIMIT_BYTES)


def _row_tile(m, cap):
    best = 16
    for t in range(16, cap + 1, 16):
        if m % t == 0:
            best = t
    return best


def _split3(x):
    h = x.astype(BF16)
    r = x - h.astype(F32)
    m = r.astype(BF16)
    l = (r - m.astype(F32)).astype(BF16)
    return h, m, l


def _dot(a, b):
    return jnp.dot(a, b, preferred_element_type=F32)


def _dot01_rhs(x, e):
    h, m, l = _split3(x)
    return _dot(h, e) + _dot(m, e) + _dot(l, e)


def _dot01_lhs(e, x):
    h, m, l = _split3(x)
    return _dot(e, h) + _dot(e, m) + _dot(e, l)


def _silu(x):
    return x * jax.nn.sigmoid(x)


def _softplus(x):
    return jnp.maximum(x, 0.0) + jnp.log1p(jnp.exp(-jnp.abs(x)))


def _lanes(rs, n):
    return jnp.concatenate([rs] * (n // LANES), axis=1)


def _norm_factors_kernel(x_ref, g_ref, xg_ref, rs_ref):
    x = x_ref[...]
    ms = jnp.mean(x * x, axis=-1, keepdims=True)
    xg_ref[...] = (x * g_ref[...]).astype(xg_ref.dtype)
    rs_ref[...] = jnp.broadcast_to(lax.rsqrt(ms + EPS), rs_ref.shape)


def _norm_factors(x, g):
    m, d = x.shape
    tm = _row_tile(m, 320)
    return pl.pallas_call(
        _norm_factors_kernel,
        grid=(m // tm,),
        in_specs=[pl.BlockSpec((tm, d), lambda i: (i, 0)), pl.BlockSpec((1, d), lambda i: (0, 0))],
        out_specs=[pl.BlockSpec((tm, d), lambda i: (i, 0)), pl.BlockSpec((tm, LANES), lambda i: (i, 0))],
        out_shape=[jax.ShapeDtypeStruct((m, d), BF16), jax.ShapeDtypeStruct((m, LANES), F32)],
        compiler_params=_params("arbitrary"),
        name="norm_factors",
    )(x, g.reshape(1, d))


def _row_scale_kernel(s_ref, o_ref, *, width):
    s = s_ref[...]
    tot = s[:, :LANES]
    for c in range(1, s.shape[1] // LANES):
        tot = tot + s[:, c * LANES:(c + 1) * LANES]
    ms = jnp.sum(tot, axis=-1, keepdims=True) * (1.0 / width)
    o_ref[...] = jnp.broadcast_to(lax.rsqrt(ms + EPS), o_ref.shape)


def _row_scale(ssq, width):
    m, n = ssq.shape
    tm = _row_tile(m, 1024)
    return pl.pallas_call(
        functools.partial(_row_scale_kernel, width=width),
        grid=(m // tm,),
        in_specs=[pl.BlockSpec((tm, n), lambda i: (i, 0))],
        out_specs=pl.BlockSpec((tm, LANES), lambda i: (i, 0)),
        out_shape=jax.ShapeDtypeStruct((m, LANES), F32),
        compiler_params=_params("arbitrary"),
        name="row_scale",
    )(ssq)


ROUND_SLAB = 1024


def _mm_kernel(*refs, nx, w_cast, w_trans, groups, epi, n_epi, next_norm):
    nw = len(w_cast)
    xs, ws = refs[:nx], refs[nx:nx + nw]
    pos = nx + nw
    es = refs[pos:pos + n_epi]
    pos += n_epi
    if next_norm:
        gn_ref, o_ref, xg_ref, ssq_ref = refs[pos:pos + 4]
        pos += 4
    else:
        o_ref = refs[pos]
        pos += 1
    scrs = refs[pos:]
    scr_of = {}
    for wi, c in enumerate(w_cast):
        if c:
            scr_of[wi] = scrs[len(scr_of)]

    def finish(accs):
        y = epi(*accs, *[e[...] for e in es])
        o_ref[...] = y.astype(o_ref.dtype)
        if next_norm:
            xg_ref[...] = (y * gn_ref[...]).astype(xg_ref.dtype)
            sq = y * y
            part = sq[:, :LANES]
            for c in range(1, sq.shape[1] // LANES):
                part = part + sq[:, c * LANES:(c + 1) * LANES]
            ssq_ref[...] = part

    def products(k0, k1, round_weights):
        accs = [None] * len(groups)
        for gi, grp in enumerate(groups):
            for xi, wi in grp:
                kb = xs[xi].shape[1]
                lo, hi = min(k0, kb), min(k1, kb)
                if lo == hi:
                    continue
                if wi in scr_of:
                    scr = scr_of[wi]
                    if w_trans[wi]:
                        if round_weights:
                            scr[:, lo:hi] = ws[wi][0, :, lo:hi].astype(BF16)
                        w = scr[:, lo:hi]
                    else:
                        if round_weights:
                            scr[lo:hi, :] = ws[wi][0, lo:hi, :].astype(BF16)
                        w = scr[lo:hi, :]
                else:
                    w = ws[wi][0, :, lo:hi] if w_trans[wi] else ws[wi][0, lo:hi, :]
                x = xs[xi][:, lo:hi]
                if w_trans[wi]:
                    d = lax.dot_general(x, w, (((1,), (1,)), ((), ())), preferred_element_type=F32)
                else:
                    d = _dot(x, w)
                accs[gi] = d if accs[gi] is None else accs[gi] + d
        return accs

    k_max = max(x.shape[1] for x in xs)
    if not scr_of:
        finish(products(0, k_max, False))
        return

    @pl.when(pl.program_id(1) == 0)
    def _():
        accs = None
        for k0 in range(0, k_max, ROUND_SLAB):
            part = products(k0, min(k0 + ROUND_SLAB, k_max), True)
            accs = part if accs is None else [a if b is None else (b if a is None else a + b)
                                              for a, b in zip(accs, part)]
        finish(accs)

    @pl.when(pl.program_id(1) != 0)
    def _():
        finish(products(0, k_max, False))


def _matmul(name, layer, xs, ws, groups, epi, epi_args, n, out_dtype, tn, tm_cap, w_trans=None, row_scale=None,
            next_gain=None):
    m = xs[0][0].shape[0]
    tm = _row_tile(m, tm_cap)
    w_trans = w_trans or (False,) * len(ws)
    in_specs = []
    for _, kb, ki in xs:
        in_specs.append(pl.BlockSpec((tm, kb), lambda j, i, ki=ki: (i, ki)))
    w_blocks = []
    for (_, kb, ki), t in zip(ws, w_trans):
        if t:
            w_blocks.append((tn, kb))
            in_specs.append(pl.BlockSpec((1, tn, kb), lambda j, i, ki=ki: (layer, j, ki)))
        else:
            w_blocks.append((kb, tn))
            in_specs.append(pl.BlockSpec((1, kb, tn), lambda j, i, ki=ki: (layer, ki, j)))
    tile = pl.BlockSpec((tm, tn), lambda j, i: (i, j))
    operands = [a for a, _, _ in xs] + [a for a, _, _ in ws] + list(epi_args)
    in_specs += [tile] * len(epi_args)
    n_epi = len(epi_args)
    if row_scale is not None:
        operands.append(row_scale)
        in_specs.append(pl.BlockSpec((tm, LANES), lambda j, i: (i, 0)))
        n_epi += 1
    out_specs, out_shape = tile, jax.ShapeDtypeStruct((m, n), out_dtype)
    if next_gain is not None:
        operands.append(next_gain.reshape(1, n))
        in_specs.append(pl.BlockSpec((1, tn), lambda j, i: (0, j)))
        out_specs = [tile, tile, pl.BlockSpec((tm, LANES), lambda j, i: (i, j))]
        out_shape = [out_shape, jax.ShapeDtypeStruct((m, n), BF16),
                     jax.ShapeDtypeStruct((m, n // tn * LANES), F32)]
    w_cast = tuple(a.dtype != BF16 for a, _, _ in ws)
    kern = functools.partial(_mm_kernel, nx=len(xs), w_cast=w_cast, w_trans=w_trans, groups=groups, epi=epi,
                             n_epi=n_epi, next_norm=next_gain is not None)
    res = pl.pallas_call(
        kern,
        grid=(n // tn, m // tm),
        in_specs=in_specs,
        out_specs=out_specs,
        out_shape=out_shape,
        scratch_shapes=[pltpu.VMEM(blk, BF16) for blk, c in zip(w_blocks, w_cast) if c],
        compiler_params=_params("arbitrary", "arbitrary"),
        name=name,
    )(*operands)
    if next_gain is None:
        return res
    out, xg, ssq = res
    return out, xg, _row_scale(ssq, n)


def _epi_swiglu(a, b, rs):
    r = _lanes(rs, a.shape[1])
    return _silu(a * r) * (b * r)


def _epi_half_residual(acc, res):
    return res + 0.5 * acc


def _epi_residual(acc, res):
    return res + acc


def _epi_scaled(acc, rs):
    return acc * _lanes(rs, acc.shape[1])


def _epi_gated(a, c, res, rs):
    return res + jax.nn.sigmoid(a * _lanes(rs, a.shape[1])) * c


def _ffn(x, xg, rs, layer, next_gain, wg, wu, wd):
    _, d, f = wg.shape
    h = _matmul("ffn_gate_up", layer, [(xg, d, 0)], [(wg, d, 0), (wu, d, 0)], [[(0, 0)], [(0, 1)]], _epi_swiglu, [],
                f, BF16, tn=256, tm_cap=1024, row_scale=rs)
    return _matmul("ffn_down", layer, [(h, f, 0)], [(wd, f, 0)], [[(0, 0)]], _epi_half_residual, [x],
                   d, F32, tn=512, tm_cap=432, next_gain=next_gain)


Q_SCALE = HEAD_DIM ** -0.5 * math.log2(math.e)


def _qk_prep_kernel(q_ref, k_ref, qg_ref, kg_ref, bd_ref, qo_ref, ko_ref):
    bd = bd_ref[...]

    def norm(x, g):
        ms = _dot01_rhs(x * x, bd) * (1.0 / HEAD_DIM)
        return x * lax.rsqrt(ms + EPS) * g

    qg = qg_ref[...]
    kg = kg_ref[...]
    for c in range(Q_COLS // LANES):
        sl = slice(c * LANES, (c + 1) * LANES)
        qo_ref[:, sl] = (norm(q_ref[:, sl], qg) * Q_SCALE).astype(qo_ref.dtype)
    for c in range(K_COLS // LANES):
        sl = slice(c * LANES, (c + 1) * LANES)
        ko_ref[:, sl] = norm(k_ref[:, sl], kg)


def _qk_prep(h, q_g, k_g):
    m = h.shape[0]
    tm = _row_tile(m, 320)
    lane = jnp.arange(LANES)
    bd = (lane[:, None] // HEAD_DIM == lane[None, :] // HEAD_DIM).astype(BF16)
    qg2 = jnp.tile(q_g.astype(F32), LANES // HEAD_DIM).reshape(1, LANES)
    kg2 = jnp.tile(k_g.astype(F32), LANES // HEAD_DIM).reshape(1, LANES)
    return pl.pallas_call(
        _qk_prep_kernel,
        grid=(m // tm,),
        in_specs=[
            pl.BlockSpec((tm, Q_COLS), lambda i: (i, 0)),
            pl.BlockSpec((tm, K_COLS), lambda i: (i, Q_COLS // K_COLS)),
            pl.BlockSpec((1, LANES), lambda i: (0, 0)),
            pl.BlockSpec((1, LANES), lambda i: (0, 0)),
            pl.BlockSpec((LANES, LANES), lambda i: (0, 0)),
        ],
        out_specs=[pl.BlockSpec((tm, Q_COLS), lambda i: (i, 0)), pl.BlockSpec((tm, K_COLS), lambda i: (i, 0))],
        out_shape=[jax.ShapeDtypeStruct((m, Q_COLS), BF16), jax.ShapeDtypeStruct((m, K_COLS), F32)],
        compiler_params=_params("arbitrary"),
        name="qk_norm",
    )(h, h, qg2, kg2, bd)


def _lambda(lq1, lk1, lq2, lk2, lam0):
    s1 = jnp.sum(lq1[...] * lk1[...], axis=-1, keepdims=True)
    s2 = jnp.sum(lq2[...] * lk2[...], axis=-1, keepdims=True)
    return jnp.exp(s1) - jnp.exp(s2) + lam0


def _sub_layer_norm(o, g, lam0):
    ms = jnp.mean(o * o, axis=-1, keepdims=True)
    return (o * lax.rsqrt(ms + EPS) * g) * (1.0 - lam0)


SUM_ROWS = 16


def _attn_prompt_kernel(lq1, lk1, lq2, lk2, sg_ref, q_ref, k_ref, v_ref, o_ref, kb_sc, vt_sc, m_sc, acc_sc,
                        *, tq, nq, lam0):
    qi = pl.program_id(2)

    @pl.when(qi == 0)
    def _():
        kb_sc[...] = k_ref[...].astype(BF16)
        ones_rows = (lax.broadcasted_iota(jnp.int32, (SUM_ROWS, tq), 0) == 0).astype(BF16)
        for j in range(nq):
            vt_sc[j, :V_DIM, :] = v_ref[j * tq:(j + 1) * tq, :].T.astype(BF16)
            vt_sc[j, V_DIM:, :] = ones_rows

    q2 = q_ref[...]
    qs = jnp.concatenate([q2[:, :LANES], q2[:, LANES:]], axis=0)
    lane = lax.broadcasted_iota(jnp.int32, qs.shape, 1)
    zero = jnp.zeros_like(qs)
    qm = (jnp.where(lane < HEAD_DIM, qs, zero), jnp.where(lane >= HEAD_DIM, qs, zero))
    m_sc[...] = jnp.full(m_sc.shape, -jnp.inf, F32)
    acc_sc[...] = jnp.zeros(acc_sc.shape, F32)

    def block(kj, masked):
        start = pl.multiple_of(kj * tq, tq)
        kb = kb_sc[pl.ds(start, tq), :]
        vt = vt_sc[kj]
        sts = [lax.dot_general(kb, qm[m], (((1,), (1,)), ((), ())), preferred_element_type=F32) for m in range(2)]
        for m in range(2):
            st = sts[m]
            if masked:
                key = lax.broadcasted_iota(jnp.int32, st.shape, 0)
                qry = lax.broadcasted_iota(jnp.int32, st.shape, 1) % tq
                st = jnp.where(key <= qry, st, NEG)
            m_prev = m_sc[m]
            m_new = jnp.maximum(m_prev, jnp.max(st, axis=0, keepdims=True))
            p = jnp.exp2(st - m_new).astype(BF16)
            acc_sc[m] = jnp.exp2(m_prev - m_new) * acc_sc[m] + _dot(vt, p)
            m_sc[m] = m_new

    def body(kj, carry):
        block(kj, False)
        return carry

    lax.fori_loop(0, qi, body, 0)
    block(qi, True)

    lam = _lambda(lq1, lk1, lq2, lk2, lam0)
    o0, o1 = acc_sc[0], acc_sc[1]
    ot = o0[:V_DIM] / o0[V_DIM:V_DIM + 1] - lam * (o1[:V_DIM] / o1[V_DIM:V_DIM + 1])
    ms = jnp.mean(ot * ot, axis=0, keepdims=True)
    on = (ot * lax.rsqrt(ms + EPS)).T
    on = ((on * sg_ref[...]) * (1.0 - lam0)).astype(o_ref.dtype)
    o_ref[:, :LANES] = on[:tq]
    o_ref[:, LANES:] = on[tq:]


def _lam_specs():
    zeros = (0,) * 2
    return [pl.BlockSpec((1, HEAD_DIM), lambda *a: zeros) for _ in range(4)] + [pl.BlockSpec((1, V_DIM), lambda *a: zeros)]


def _attn_prompt(qn, kn, h, lam_vecs, subln_g, lam0, batch, seq):
    tq = next((t for t in (512, 256) if seq % t == 0), seq)
    nq = seq // tq
    kern = functools.partial(_attn_prompt_kernel, tq=tq, nq=nq, lam0=lam0)
    v_blk = (Q_COLS + K_COLS) // V_DIM
    return pl.pallas_call(
        kern,
        grid=(batch, KV_HEADS, nq),
        in_specs=_lam_specs() + [
            pl.BlockSpec((tq, 2 * LANES), lambda b, g, i: (b * nq + i, g)),
            pl.BlockSpec((seq, LANES), lambda b, g, i: (b, g)),
            pl.BlockSpec((seq, V_DIM), lambda b, g, i: (b, v_blk + g)),
        ],
        out_specs=pl.BlockSpec((tq, 2 * V_DIM), lambda b, g, i: (b * nq + i, g)),
        out_shape=jax.ShapeDtypeStruct((batch * seq, ATT_WIDTH), BF16),
        scratch_shapes=[pltpu.VMEM((seq, LANES), BF16), pltpu.VMEM((nq, V_DIM + SUM_ROWS, tq), BF16),
                        pltpu.VMEM((2, 1, 2 * tq), F32), pltpu.VMEM((2, V_DIM + SUM_ROWS, 2 * tq), F32)],
        compiler_params=_params("arbitrary", "arbitrary", "arbitrary"),
        name="attn_prompt",
    )(*lam_vecs, subln_g.reshape(1, V_DIM), qn, kn, h)


DECODE_ROWS = 2 * KV_HEADS * KV_REP


def _attn_decode_kernel(pt_ref, lq1, lk1, lq2, lk2, sg_ref, q_ref, kn_ref, vn_ref, *refs, pages, lam0):
    ck_refs, cv_refs = refs[:pages], refs[pages:2 * pages]
    o_ref, m_sc, l_sc, acc_sc = refs[2 * pages:]
    p = pl.program_id(1)
    page_rows = PAGE_SIZE * KV_HEADS

    @pl.when(p == 0)
    def _():
        m_sc[...] = jnp.full(m_sc.shape, -jnp.inf, F32)
        l_sc[...] = jnp.zeros(l_sc.shape, F32)
        acc_sc[...] = jnp.zeros(acc_sc.shape, F32)

    q = q_ref[0]
    qb = q.astype(BF16)
    row_head = (lax.broadcasted_iota(jnp.int32, (DECODE_ROWS, page_rows), 0) % (KV_HEADS * KV_REP)) // KV_REP
    col_head = lax.broadcasted_iota(jnp.int32, (DECODE_ROWS, page_rows), 1) % KV_HEADS
    same_head = row_head == col_head

    scores, values = [], []
    for j in range(pages):
        kb = ck_refs[j][0, 0].astype(BF16)
        s = lax.dot_general(qb, kb, (((1,), (1,)), ((), ())), preferred_element_type=F32)
        scores.append(jnp.where(same_head, s, NEG))
        values.append(cv_refs[j][0, 0].astype(BF16))
    m_prev = m_sc[...]
    m_new = m_prev
    for s in scores:
        m_new = jnp.maximum(m_new, jnp.max(s, axis=-1, keepdims=True))
    alpha = jnp.exp2(m_prev - m_new)
    l_new = alpha * l_sc[...]
    acc = alpha * acc_sc[...]
    for s, vb in zip(scores, values):
        pr = jnp.exp2(s - m_new)
        l_new = l_new + jnp.sum(pr, axis=-1, keepdims=True)
        acc = acc + _dot(pr.astype(BF16), vb)
    m_sc[...] = m_new
    l_sc[...] = l_new
    acc_sc[...] = acc

    @pl.when(p == pl.num_programs(1) - 1)
    def _():
        s = jnp.sum(q * kn_ref[0], axis=-1, keepdims=True)
        m_fin = jnp.maximum(m_new, s)
        a = jnp.exp2(m_new - m_fin)
        pr = jnp.exp2(s - m_fin)
        l_fin = a * l_new + pr
        o = (a * acc + pr * vn_ref[0]) / l_fin
        half = DECODE_ROWS // 2
        lam = _lambda(lq1, lk1, lq2, lk2, lam0)
        o_ref[0] = _sub_layer_norm(o[:half] - lam * o[half:], sg_ref[...], lam0)


def _attn_decode(qs, ks, vs, cache_k, cache_v, page_table, layer, lam_vecs, subln_g, lam0):
    db, n_pages = page_table.shape
    depth, n_phys = cache_k.shape[:2]
    pages = next(k for k in (8, 4, 2, 1) if n_pages % k == 0)
    page_rows = PAGE_SIZE * KV_HEADS
    ck = cache_k.reshape(depth, n_phys, page_rows, 2 * HEAD_DIM)
    cv = cache_v.reshape(depth, n_phys, page_rows, V_DIM)
    q5 = qs.reshape(db, KV_HEADS, KV_REP, 2, HEAD_DIM)
    z = jnp.zeros((db, KV_HEADS, KV_REP, HEAD_DIM), F32)
    qall = jnp.stack([jnp.concatenate([q5[:, :, :, 0], z], -1), jnp.concatenate([z, q5[:, :, :, 1]], -1)], axis=1)
    qall = qall.reshape(db, DECODE_ROWS, LANES)
    per_row = lambda t: jnp.broadcast_to(t.reshape(db, 1, KV_HEADS, 1, LANES),
                                         (db, 2, KV_HEADS, KV_REP, LANES)).reshape(db, DECODE_ROWS, LANES)
    kern = functools.partial(_attn_decode_kernel, pages=pages, lam0=lam0)
    row = lambda b, p, pt: (b, 0, 0)
    page = lambda j: (lambda b, p, pt: (layer, pt[b, p * pages + j], 0, 0))
    grid_spec = pltpu.PrefetchScalarGridSpec(
        num_scalar_prefetch=1,
        grid=(db, n_pages // pages),
        in_specs=_lam_specs() + [pl.BlockSpec((1, DECODE_ROWS, LANES), row) for _ in range(3)]
        + [pl.BlockSpec((1, 1, page_rows, 2 * HEAD_DIM), page(j)) for j in range(pages)]
        + [pl.BlockSpec((1, 1, page_rows, V_DIM), page(j)) for j in range(pages)],
        out_specs=pl.BlockSpec((1, DECODE_ROWS // 2, V_DIM), row),
        scratch_shapes=[pltpu.VMEM((DECODE_ROWS, 1), F32), pltpu.VMEM((DECODE_ROWS, 1), F32),
                        pltpu.VMEM((DECODE_ROWS, V_DIM), F32)],
    )
    out = pl.pallas_call(
        kern,
        grid_spec=grid_spec,
        out_shape=jax.ShapeDtypeStruct((db, DECODE_ROWS // 2, V_DIM), F32),
        compiler_params=_params("arbitrary", "arbitrary"),
        name="attn_decode",
    )(page_table, *lam_vecs, subln_g.reshape(1, V_DIM), qall, per_row(ks), per_row(vs),
      *([ck] * pages), *([cv] * pages))
    return out.reshape(db, ATT_WIDTH)


def _gate_norm(y, xs_c, z, d_e, g_e):
    y = (y + d_e * xs_c) * _silu(z)
    outs = []
    for g in range(SSM_GROUPS):
        yg = y[:, g * SSM_GROUP_W:(g + 1) * SSM_GROUP_W]
        ms = jnp.mean(yg * yg, axis=-1, keepdims=True)
        outs.append(yg * lax.rsqrt(ms + EPS) * g_e[:, g * SSM_GROUP_W:(g + 1) * SSM_GROUP_W])
    return outs


def _dt_raw(xg, rs, wdt_ref):
    w = wdt_ref[0].astype(BF16)
    w = jnp.concatenate([w, jnp.zeros((LANES - SSM_HEADS, w.shape[1]), BF16)], axis=0)
    return lax.dot_general(xg, w, (((1,), (1,)), ((), ())), preferred_element_type=F32) * rs


def _dt_weight_spec(layer):
    return pl.BlockSpec((1, SSM_HEADS, D_MODEL), lambda *a: (layer, DT_OFF // SSM_HEADS, 0))


def _ssd_prompt_kernel(z_ref, x_ref, b_ref, c_ref, xg_ref, rs_ref, wdt_ref, cwx, cwb, cwc, cbx, cbb, cbc, dtb_c, alog_c,
                       d_e, g_e, tril_ref, exp_ref, o_ref, s_ref, xpx, xpb, xpc, st_sc, y_sc):
    c = pl.program_id(1)
    q = SSD_CHUNK

    @pl.when(c == 0)
    def _():
        xpx[0:8, :] = jnp.zeros((8, xpx.shape[1]), F32)
        xpb[0:8, :] = jnp.zeros((8, xpb.shape[1]), F32)
        xpc[0:8, :] = jnp.zeros((8, xpc.shape[1]), F32)
        st_sc[...] = jnp.zeros(st_sc.shape, F32)

    def conv(src, pad, w, b):
        pad[8:8 + q, :] = src[...]
        y = b[...]
        for t in range(CONV_WIDTH):
            y = y + w[t:t + 1, :] * pad[5 + t:5 + t + q, :]
        tail = pad[q:q + 8, :]
        pad[0:8, :] = tail
        return _silu(y)

    xs_c = conv(x_ref, xpx, cwx, cbx)
    bc = conv(b_ref, xpb, cwb, cbb)
    cc = conv(c_ref, xpc, cwc, cbc)

    tril = tril_ref[...]
    dtr = _dt_raw(xg_ref[...], rs_ref[...], wdt_ref)
    dt_c = _softplus(dtr + dtb_c[...])
    acs_c = _dot01_lhs(tril, dt_c * (-jnp.exp(alog_c[...])))
    acs_t = acs_c.T
    dt_e = _dot01_rhs(dt_c, exp_ref[...])
    acs_e = _dot01_rhs(acs_c, exp_ref[...])
    a_last = acs_e[q - 1:q, :]
    xdt = xs_c * dt_e
    xdtd = (xdt * jnp.exp(a_last - acs_e)).astype(BF16)
    xdt = xdt.astype(BF16)
    ea = jnp.exp(acs_e)
    dec_end = jnp.exp(a_last)

    ii = lax.broadcasted_iota(jnp.int32, (q, q), 0)
    jj = lax.broadcasted_iota(jnp.int32, (q, q), 1)
    causal = jj <= ii
    lane_lo = jj < SSM_HEAD_DIM

    for g in range(SSM_GROUPS):
        gs = slice(g * SSM_GROUP_W, (g + 1) * SSM_GROUP_W)
        bg = bc[:, g * SSM_STATE:(g + 1) * SSM_STATE].astype(BF16)
        cg = cc[:, g * SSM_STATE:(g + 1) * SSM_STATE].astype(BF16)
        cb = lax.dot_general(cg, bg, (((1,), (1,)), ((), ())), preferred_element_type=F32)
        st = st_sc[g]
        y_off = _dot(cg, st.astype(BF16)) * ea[:, gs]
        pairs = []
        for pr in range(SSM_HPG // 2):
            col = g * SSM_GROUP_W + pr * LANES
            xpair = xdt[:, col:col + LANES]
            ys = []
            for hh in range(2):
                head = g * SSM_HPG + pr * 2 + hh
                seg = acs_c[:, head:head + 1] - acs_t[head:head + 1, :]
                lmat = jnp.where(causal, jnp.exp(seg), 0.0)
                ys.append(_dot((cb * lmat).astype(BF16), xpair))
            pairs.append(jnp.where(lane_lo, ys[0], ys[1]))
        y_sc[:, gs] = jnp.concatenate(pairs, axis=1) + y_off
        upd = lax.dot_general(bg, xdtd[:, gs], (((0,), (0,)), ((), ())), preferred_element_type=F32)
        st_sc[g] = st * dec_end[:, gs] + upd

    outs = _gate_norm(y_sc[...], xs_c, z_ref[...], d_e[...], g_e[...])
    for g in range(SSM_GROUPS):
        o_ref[:, g * SSM_GROUP_W:(g + 1) * SSM_GROUP_W] = outs[g].astype(o_ref.dtype)

    @pl.when(c == pl.num_programs(1) - 1)
    def _():
        for g in range(SSM_GROUPS):
            t = st_sc[g].T
            for r in range(SSM_HPG):
                s_ref[0, g * SSM_HPG + r] = t[r * SSM_HEAD_DIM:(r + 1) * SSM_HEAD_DIM, :]


def _ssm_consts(conv_w, conv_b, dt_bias, a_log, d_skip, ssm_g):
    rep = lambda v: jnp.repeat(v.astype(F32), SSM_HEAD_DIM).reshape(1, SSM_WIDTH)
    pad = lambda v: jnp.pad(v.astype(F32), (0, LANES - SSM_HEADS)).reshape(1, LANES)
    head_of_col = jnp.arange(SSM_WIDTH) // SSM_HEAD_DIM
    expand = (jnp.arange(LANES)[:, None] == head_of_col[None, :]).astype(BF16)
    tril = (jnp.arange(SSD_CHUNK)[None, :] <= jnp.arange(SSD_CHUNK)[:, None]).astype(BF16)
    return dict(conv_w=conv_w.astype(F32), conv_b=conv_b.astype(F32).reshape(1, CONV_CH),
                dtb_c=pad(dt_bias), alog_c=pad(a_log),
                d_e=rep(d_skip), g_e=ssm_g.astype(F32).reshape(1, SSM_WIDTH), tril=tril, expand=expand)


def _conv_param_specs(zmap):
    n_x, n_b = SSM_WIDTH, SSM_GROUPS * SSM_STATE
    cmap = lambda k: (lambda *a: (0, k))
    return [pl.BlockSpec((CONV_WIDTH, n_x), cmap(0)), pl.BlockSpec((CONV_WIDTH, n_b), cmap(2)),
            pl.BlockSpec((CONV_WIDTH, n_b), cmap(3)),
            pl.BlockSpec((1, n_x), cmap(0)), pl.BlockSpec((1, n_b), cmap(2)), pl.BlockSpec((1, n_b), cmap(3))]


def _ssd_prompt(h, xg, rs, w_dt, layer, sc, batch, seq):
    q = SSD_CHUNK
    nc = seq // q
    rb = lambda b, c: b * nc + c
    n_b = SSM_GROUPS * SSM_STATE
    zmap = lambda b, c: (0, 0)
    vec = lambda n: pl.BlockSpec((1, n), zmap)
    in_specs = [
        pl.BlockSpec((q, SSM_WIDTH), lambda b, c: (rb(b, c), Z_OFF // SSM_WIDTH)),
        pl.BlockSpec((q, SSM_WIDTH), lambda b, c: (rb(b, c), X_OFF // SSM_WIDTH)),
        pl.BlockSpec((q, n_b), lambda b, c: (rb(b, c), B_OFF // n_b)),
        pl.BlockSpec((q, n_b), lambda b, c: (rb(b, c), C_OFF // n_b)),
        pl.BlockSpec((q, D_MODEL), lambda b, c: (rb(b, c), 0)),
        pl.BlockSpec((q, LANES), lambda b, c: (rb(b, c), 0)),
        _dt_weight_spec(layer),
    ] + _conv_param_specs(zmap) + [
        vec(LANES), vec(LANES), vec(SSM_WIDTH), vec(SSM_WIDTH),
        pl.BlockSpec((q, q), zmap), pl.BlockSpec((LANES, SSM_WIDTH), zmap),
    ]
    out, state = pl.pallas_call(
        _ssd_prompt_kernel,
        grid=(batch, nc),
        in_specs=in_specs,
        out_specs=[pl.BlockSpec((q, SSM_WIDTH), lambda b, c: (rb(b, c), 0)),
                   pl.BlockSpec((1, SSM_HEADS, SSM_HEAD_DIM, SSM_STATE), lambda b, c: (b, 0, 0, 0))],
        out_shape=[jax.ShapeDtypeStruct((batch * seq, SSM_WIDTH), BF16),
                   jax.ShapeDtypeStruct((batch, SSM_HEADS, SSM_HEAD_DIM, SSM_STATE), F32)],
        scratch_shapes=[pltpu.VMEM((q + 8, SSM_WIDTH), F32), pltpu.VMEM((q + 8, n_b), F32),
                        pltpu.VMEM((q + 8, n_b), F32),
                        pltpu.VMEM((SSM_GROUPS, SSM_STATE, SSM_GROUP_W), F32), pltpu.VMEM((q, SSM_WIDTH), F32)],
        compiler_params=_params("arbitrary", "arbitrary"),
        name="ssd_prompt",
    )(h, h, h, h, xg, rs, w_dt, sc["conv_w"], sc["conv_w"], sc["conv_w"], sc["conv_b"], sc["conv_b"], sc["conv_b"],
      sc["dtb_c"], sc["alog_c"], sc["d_e"], sc["g_e"], sc["tril"], sc["expand"])
    return out, state


def _ssm_decode_pre_kernel(x_ref, b_ref, c_ref, xg_ref, rs_ref, wdt_ref, prev_ref, cwx, cwb, cwc, cbx, cbb, cbc, dtb_c, alog_c,
                           exp_ref, xs_o, b_o, c_o, xdt_o, dec_o):
    def conv(src, col, w, b):
        n = src.shape[1]
        y = b[...] + w[CONV_WIDTH - 1:CONV_WIDTH, :] * src[...]
        for t in range(CONV_WIDTH - 1):
            y = y + w[t:t + 1, :] * prev_ref[t, :, col:col + n]
        return _silu(y)

    xs_c = conv(x_ref, 0, cwx, cbx)
    xs_o[...] = xs_c
    b_o[...] = conv(b_ref, SSM_WIDTH, cwb, cbb)
    c_o[...] = conv(c_ref, SSM_WIDTH + SSM_GROUPS * SSM_STATE, cwc, cbc)
    dtr = _dt_raw(xg_ref[...], rs_ref[...], wdt_ref)[:x_ref.shape[0]]
    dt_c = _softplus(dtr + dtb_c[...])
    dec_o[...] = jnp.exp(dt_c * (-jnp.exp(alog_c[...])))
    xdt_o[...] = xs_c * _dot01_rhs(dt_c, exp_ref[...])


def _ssm_decode_state_kernel(s_ref, xdt_ref, dec_ref, b_ref, c_ref, so_ref, y_ref):
    for head in range(SSM_HEADS):
        g = head // SSM_HPG
        s = dec_ref[0, head] * s_ref[0, 0, head] + xdt_ref[0, head] * b_ref[0, g]
        so_ref[0, head] = s
        y_ref[0, head] = jnp.sum(s * c_ref[0, g], axis=-1, keepdims=True)


def _ssm_decode_post_kernel(y_ref, xs_ref, z_ref, d_e, g_e, o_ref):
    outs = _gate_norm(y_ref[...], xs_ref[...], z_ref[...], d_e[...], g_e[...])
    o_ref[...] = jnp.zeros(o_ref.shape, o_ref.dtype)
    db = y_ref.shape[0]
    for g in range(SSM_GROUPS):
        o_ref[0:db, g * SSM_GROUP_W:(g + 1) * SSM_GROUP_W] = outs[g]


def _ssm_decode(h, xg, rs, w_dt, sc, state_conv_l, state_ssm, layer, row0, db):
    n_b = SSM_GROUPS * SSM_STATE
    rblk = row0 // db
    zmap = lambda i: (0, 0)
    vec = lambda n: pl.BlockSpec((1, n), zmap)
    prev = jnp.transpose(state_conv_l, (1, 0, 2)).astype(F32)
    f = lambda n: jax.ShapeDtypeStruct((db, n), F32)
    xs_c, b_c, c_c, xdt, dec = pl.pallas_call(
        _ssm_decode_pre_kernel,
        grid=(1,),
        in_specs=[
            pl.BlockSpec((db, SSM_WIDTH), lambda i: (rblk, X_OFF // SSM_WIDTH)),
            pl.BlockSpec((db, n_b), lambda i: (rblk, B_OFF // n_b)),
            pl.BlockSpec((db, n_b), lambda i: (rblk, C_OFF // n_b)),
            pl.BlockSpec((PAD_ROWS, D_MODEL), lambda i: (row0 // PAD_ROWS, 0)),
            pl.BlockSpec((PAD_ROWS, LANES), lambda i: (row0 // PAD_ROWS, 0)),
            _dt_weight_spec(layer),
            pl.BlockSpec((CONV_WIDTH - 1, db, CONV_CH), lambda i: (0, 0, 0)),
        ] + _conv_param_specs(zmap) + [vec(LANES), vec(LANES),
                                       pl.BlockSpec((LANES, SSM_WIDTH), zmap)],
        out_specs=[pl.BlockSpec((db, SSM_WIDTH), zmap), pl.BlockSpec((db, n_b), zmap), pl.BlockSpec((db, n_b), zmap),
                   pl.BlockSpec((db, SSM_WIDTH), zmap), pl.BlockSpec((db, LANES), zmap)],
        out_shape=[f(SSM_WIDTH), f(n_b), f(n_b), f(SSM_WIDTH), f(LANES)],
        compiler_params=_params("arbitrary"),
        name="ssm_decode_pre",
    )(h, h, h, xg, rs, w_dt, prev, sc["conv_w"], sc["conv_w"], sc["conv_w"], sc["conv_b"], sc["conv_b"], sc["conv_b"],
      sc["dtb_c"], sc["alog_c"], sc["expand"])

    xdt4 = xdt.reshape(db, SSM_HEADS, SSM_HEAD_DIM, 1)
    dec4 = dec[:, :SSM_HEADS].reshape(db, SSM_HEADS, 1, 1)
    b4 = b_c.reshape(db, SSM_GROUPS, 1, SSM_STATE)
    c4 = c_c.reshape(db, SSM_GROUPS, 1, SSM_STATE)
    s_new, y4 = pl.pallas_call(
        _ssm_decode_state_kernel,
        grid=(db,),
        in_specs=[
            pl.BlockSpec((1, 1, SSM_HEADS, SSM_HEAD_DIM, SSM_STATE), lambda b: (layer, b, 0, 0, 0)),
            pl.BlockSpec((1, SSM_HEADS, SSM_HEAD_DIM, 1), lambda b: (b, 0, 0, 0)),
            pl.BlockSpec((1, SSM_HEADS, 1, 1), lambda b: (b, 0, 0, 0)),
            pl.BlockSpec((1, SSM_GROUPS, 1, SSM_STATE), lambda b: (b, 0, 0, 0)),
            pl.BlockSpec((1, SSM_GROUPS, 1, SSM_STATE), lambda b: (b, 0, 0, 0)),
        ],
        out_specs=[pl.BlockSpec((1, SSM_HEADS, SSM_HEAD_DIM, SSM_STATE), lambda b: (b, 0, 0, 0)),
                   pl.BlockSpec((1, SSM_HEADS, SSM_HEAD_DIM, 1), lambda b: (b, 0, 0, 0))],
        out_shape=[jax.ShapeDtypeStruct((db, SSM_HEADS, SSM_HEAD_DIM, SSM_STATE), F32),
                   jax.ShapeDtypeStruct((db, SSM_HEADS, SSM_HEAD_DIM, 1), F32)],
        compiler_params=_params("arbitrary"),
        name="ssm_decode_state",
    )(state_ssm, xdt4, dec4, b4, c4)

    y = y4.reshape(db, SSM_WIDTH)
    out = pl.pallas_call(
        _ssm_decode_post_kernel,
        grid=(1,),
        in_specs=[pl.BlockSpec((db, SSM_WIDTH), zmap), pl.BlockSpec((db, SSM_WIDTH), zmap),
                  pl.BlockSpec((db, SSM_WIDTH), lambda i: (rblk, Z_OFF // SSM_WIDTH)),
                  vec(SSM_WIDTH), vec(SSM_WIDTH)],
        out_specs=pl.BlockSpec((PAD_ROWS, SSM_WIDTH), zmap),
        out_shape=jax.ShapeDtypeStruct((PAD_ROWS, SSM_WIDTH), F32),
        compiler_params=_params("arbitrary"),
        name="ssm_decode_post",
    )(y, xs_c, h, sc["d_e"], sc["g_e"])
    return out, s_new


def _lambda_init(layer):
    return 0.8 - 0.6 * math.exp(-0.3 * layer)


def _token_mixing(x, xg, rs, layer, batch, seq, db, w, cache_k, cache_v, state_conv, state_ssm, page_table):
    n_prompt = batch * seq
    lam0 = _lambda_init(layer)
    w_dt = w["w_in_t"]
    h = _matmul("in_proj", layer, [(xg, D_MODEL, 0)], [(w_dt, D_MODEL, 0)], [[(0, 0)]], _epi_scaled, [],
                DT_OFF, F32, tn=512, tm_cap=1024, w_trans=(True,), row_scale=rs)
    qn, kn = _qk_prep(h, w["q_norm"], w["k_norm"])
    lam_vecs = [w[k].astype(F32).reshape(1, HEAD_DIM) for k in ("lambda_q1", "lambda_k1", "lambda_q2", "lambda_k2")]
    dec = slice(n_prompt, n_prompt + db)
    v_off = Q_COLS + K_COLS

    att = _attn_prompt(qn, kn, h, lam_vecs, w["attn_subln"], lam0, batch, seq)
    sc = _ssm_consts(w["conv_w"], w["conv_b"], w["dt_bias"], w["a_log"], w["d_skip"], w["ssm_norm"])
    ssm, ssm_state_p = _ssd_prompt(h, xg, rs, w_dt, layer, sc, batch, seq)

    qs = qn[dec].astype(F32)
    ks = kn[dec]
    vs = h[dec, v_off:v_off + V_COLS]
    att_s = _attn_decode(qs, ks, vs, cache_k, cache_v, page_table, layer, lam_vecs, w["attn_subln"], lam0)
    att = jnp.concatenate([att, jnp.pad(att_s, ((0, PAD_ROWS - db), (0, 0))).astype(BF16)], axis=0)
    ssm_s, ssm_state_s = _ssm_decode(h, xg, rs, w_dt, sc, state_conv[layer], state_ssm, layer, n_prompt, db)
    ssm = jnp.concatenate([ssm, ssm_s.astype(BF16)], axis=0)

    half = ATT_WIDTH
    x, xg, rs = _matmul("out_proj", layer, [(att, half, 0), (ssm, half, 0)],
                        [(w["w_out"], half, 0), (w["w_out"], half, 1)], [[(0, 0), (1, 1)]], _epi_residual, [x],
                        D_MODEL, F32, tn=512, tm_cap=1024, next_gain=w["ffn2_norm"])

    k_p = kn[:n_prompt].reshape(batch, seq, KV_HEADS, 2 * HEAD_DIM)
    v_p = h[:n_prompt, v_off:v_off + V_COLS].reshape(batch, seq, KV_HEADS, V_DIM)
    conv_p = jnp.stack([h[(b + 1) * seq - (CONV_WIDTH - 1):(b + 1) * seq, X_OFF:X_OFF + CONV_CH] for b in range(batch)])
    k_s = ks.reshape(db, 1, KV_HEADS, 2 * HEAD_DIM)
    v_s = vs.reshape(db, 1, KV_HEADS, V_DIM)
    conv_s = jnp.concatenate([state_conv[layer][:, 1:].astype(F32), h[dec, X_OFF:X_OFF + CONV_CH][:, None, :]], axis=1)
    return x, xg, rs, (k_p, v_p, conv_p, ssm_state_p, k_s, v_s, conv_s, ssm_state_s)


def _ple(x, xg, rs, layer, next_gain, p, w_gate, w_proj):
    return _matmul("ple", layer, [(xg, D_MODEL, 0), (p, PLE_DIM, 0)], [(w_gate, D_MODEL, 0), (w_proj, PLE_DIM, 0)],
                   [[(0, 0)], [(1, 1)]], _epi_gated, [x], D_MODEL, F32, tn=512, tm_cap=1024, row_scale=rs,
                   next_gain=next_gain)


def kernel(x_prompt, x_sample, cache_k, cache_v, state_conv, state_ssm, page_table, p_prompt, p_sample, ffn1_norm, ffn1_w_gate, ffn1_w_up, ffn1_w_down, mix_norm, w_in, q_norm, k_norm, lambda_q1, lambda_k1, lambda_q2, lambda_k2, attn_subln, conv_w, conv_b, dt_bias, a_log, d_skip, ssm_norm, w_out, ffn2_norm, ffn2_w_gate, ffn2_w_up, ffn2_w_down, ple_norm, ple_w_gate, ple_w_proj):
    batch, seq, d = x_prompt.shape
    db = x_sample.shape[0]
    depth = w_in.shape[0]
    n_prompt = batch * seq
    x = jnp.concatenate([x_prompt.reshape(n_prompt, d), x_sample.reshape(db, d),
                         jnp.zeros((PAD_ROWS - db, d), F32)], axis=0)
    p_all = jnp.concatenate([p_prompt.reshape(depth, n_prompt, PLE_DIM), p_sample.reshape(depth, db, PLE_DIM),
                             jnp.zeros((depth, PAD_ROWS - db, PLE_DIM), F32)], axis=1).astype(BF16)
    wd1 = ffn1_w_down.astype(BF16)
    wd2 = ffn2_w_down.astype(BF16)
    w_in_t = jnp.swapaxes(w_in, 1, 2)
    outs = []
    xg, rs = _norm_factors(x, ffn1_norm[0])
    for i in range(depth):
        x, xg, rs = _ffn(x, xg, rs, i, mix_norm[i], ffn1_w_gate, ffn1_w_up, wd1)
        w = dict(ffn2_norm=ffn2_norm[i], w_in_t=w_in_t,
                 q_norm=q_norm[i], k_norm=k_norm[i], lambda_q1=lambda_q1[i], lambda_k1=lambda_k1[i],
                 lambda_q2=lambda_q2[i], lambda_k2=lambda_k2[i], attn_subln=attn_subln[i], conv_w=conv_w[i],
                 conv_b=conv_b[i], dt_bias=dt_bias[i], a_log=a_log[i], d_skip=d_skip[i], ssm_norm=ssm_norm[i],
                 w_out=w_out)
        x, xg, rs, o = _token_mixing(x, xg, rs, i, batch, seq, db, w, cache_k, cache_v, state_conv, state_ssm,
                                     page_table)
        outs.append(o)
        x, xg, rs = _ffn(x, xg, rs, i, ple_norm[i], ffn2_w_gate, ffn2_w_up, wd2)
        if i + 1 < depth:
            x, xg, rs = _ple(x, xg, rs, i, ffn1_norm[i + 1], p_all[i], ple_w_gate, ple_w_proj)
        else:
            x = _ple(x, xg, rs, i, None, p_all[i], ple_w_gate, ple_w_proj)
    stacked = [jnp.stack([o[k] for o in outs]) for k in range(8)]
    y_prompt = x[:n_prompt].reshape(batch, seq, d)
    y_sample = x[n_prompt:n_prompt + db].reshape(db, 1, d)
    return (y_prompt, y_sample, *stacked)
```

```python
import functools
import math

import jax
import jax.numpy as jnp
from jax import lax
from jax.experimental import pallas as pl
from jax.experimental.pallas import tpu as pltpu

F32 = jnp.float32
BF16 = jnp.bfloat16

D_MODEL = 4096
PAGE_SIZE = 128
HEAD_DIM = 64
KV_HEADS = 8
KV_REP = 2
V_DIM = 128
ATT_WIDTH = KV_HEADS * KV_REP * V_DIM
Q_COLS = KV_HEADS * KV_REP * 2 * HEAD_DIM
K_COLS = KV_HEADS * 2 * HEAD_DIM
V_COLS = KV_HEADS * V_DIM
SSM_WIDTH = 2048
SSM_HEAD_DIM = 64
SSM_HEADS = 32
SSM_GROUPS = 8
SSM_HPG = 4
SSM_STATE = 128
SSM_GROUP_W = SSM_WIDTH // SSM_GROUPS
CONV_WIDTH = 4
CONV_CH = SSM_WIDTH + 2 * SSM_GROUPS * SSM_STATE
SSD_CHUNK = 128
D_FF = 11008
PLE_DIM = 256
EPS = 1e-6
NEG = -1e30
Z_OFF = Q_COLS + K_COLS + V_COLS
X_OFF = Z_OFF + SSM_WIDTH
B_OFF = X_OFF + SSM_WIDTH
C_OFF = B_OFF + SSM_GROUPS * SSM_STATE
DT_OFF = C_OFF + SSM_GROUPS * SSM_STATE
IN_COLS = DT_OFF + SSM_HEADS
PAD_ROWS = 16
LANES = 128
VMEM_LIMIT_BYTES = 56 * 1024 * 1024


def _params(*sem):
    return pltpu.CompilerParams(dimension_semantics=sem, vmem_limit_bytes=VMEM_LIMIT_BYTES)


def _row_tile(m, cap):
    best = 16
    for t in range(16, cap + 1, 16):
        if m % t == 0:
            best = t
    return best


def _split3(x):
    h = x.astype(BF16)
    r = x - h.astype(F32)
    m = r.astype(BF16)
    l = (r - m.astype(F32)).astype(BF16)
    return h, m, l


def _dot(a, b):
    return jnp.dot(a, b, preferred_element_type=F32)


def _dot01_rhs(x, e, parts=3):
    acc = None
    for part in _split3(x)[:parts]:
        d = _dot(part, e)
        acc = d if acc is None else acc + d
    return acc


def _dot01_lhs(e, x):
    h, m, l = _split3(x)
    return _dot(e, h) + _dot(e, m) + _dot(e, l)


def _silu(x):
    return x * jax.nn.sigmoid(x)


def _softplus(x):
    return jnp.maximum(x, 0.0) + jnp.log1p(jnp.exp(-jnp.abs(x)))


def _lanes(rs, n):
    return jnp.concatenate([rs] * (n // LANES), axis=1)


def _norm_factors_kernel(x_ref, g_ref, xg_ref, rs_ref):
    x = x_ref[...]
    ms = jnp.mean(x * x, axis=-1, keepdims=True)
    xg_ref[...] = (x * g_ref[...]).astype(xg_ref.dtype)
    rs_ref[...] = jnp.broadcast_to(lax.rsqrt(ms + EPS), rs_ref.shape)


def _norm_factors(x, g):
    m, d = x.shape
    tm = _row_tile(m, 320)
    return pl.pallas_call(
        _norm_factors_kernel,
        grid=(m // tm,),
        in_specs=[pl.BlockSpec((tm, d), lambda i: (i, 0)), pl.BlockSpec((1, d), lambda i: (0, 0))],
        out_specs=[pl.BlockSpec((tm, d), lambda i: (i, 0)), pl.BlockSpec((tm, LANES), lambda i: (i, 0))],
        out_shape=[jax.ShapeDtypeStruct((m, d), BF16), jax.ShapeDtypeStruct((m, LANES), F32)],
        compiler_params=_params("arbitrary"),
        name="norm_factors",
    )(x, g.reshape(1, d))


def _row_scale_kernel(s_ref, o_ref, *, width):
    s = s_ref[...]
    tot = s[:, :LANES]
    for c in range(1, s.shape[1] // LANES):
        tot = tot + s[:, c * LANES:(c + 1) * LANES]
    ms = jnp.sum(tot, axis=-1, keepdims=True) * (1.0 / width)
    o_ref[...] = jnp.broadcast_to(lax.rsqrt(ms + EPS), o_ref.shape)


def _row_scale(ssq, width):
    m, n = ssq.shape
    tm = _row_tile(m, 1024)
    return pl.pallas_call(
        functools.partial(_row_scale_kernel, width=width),
        grid=(m // tm,),
        in_specs=[pl.BlockSpec((tm, n), lambda i: (i, 0))],
        out_specs=pl.BlockSpec((tm, LANES), lambda i: (i, 0)),
        out_shape=jax.ShapeDtypeStruct((m, LANES), F32),
        compiler_params=_params("arbitrary"),
        name="row_scale",
    )(ssq)


ROUND_SLAB = 1024
TALL_ROWS = 2736


def _mm_kernel(*refs, nx, w_cast, w_trans, groups, epi, n_epi, next_norm, x_resident):
    nw = len(w_cast)
    xs, ws = refs[:nx], refs[nx:nx + nw]
    pos = nx + nw
    es = refs[pos:pos + n_epi]
    pos += n_epi
    if next_norm:
        gn_ref, o_ref, xg_ref, ssq_ref = refs[pos:pos + 4]
        pos += 4
    else:
        o_ref = refs[pos]
        pos += 1
    scrs = refs[pos:]
    scr_of = {}
    for wi, c in enumerate(w_cast):
        if c:
            scr_of[wi] = scrs[len(scr_of)]

    def finish(accs):
        y = epi(*accs, *[e[...] for e in es])
        o_ref[...] = y.astype(o_ref.dtype)
        if next_norm:
            xg_ref[...] = (y * gn_ref[...]).astype(xg_ref.dtype)
            sq = y * y
            part = sq[:, :LANES]
            for c in range(1, sq.shape[1] // LANES):
                part = part + sq[:, c * LANES:(c + 1) * LANES]
            ssq_ref[...] = part

    def products(k0, k1, round_weights):
        accs = [None] * len(groups)
        for gi, grp in enumerate(groups):
            for xi, wi in grp:
                kb = xs[xi].shape[1]
                lo, hi = min(k0, kb), min(k1, kb)
                if lo == hi:
                    continue
                if wi in scr_of:
                    scr = scr_of[wi]
                    if w_trans[wi]:
                        if round_weights:
                            scr[:, lo:hi] = ws[wi][0, :, lo:hi].astype(BF16)
                        w = scr[:, lo:hi]
                    else:
                        if round_weights:
                            scr[lo:hi, :] = ws[wi][0, lo:hi, :].astype(BF16)
                        w = scr[lo:hi, :]
                else:
                    w = ws[wi][0, :, lo:hi] if w_trans[wi] else ws[wi][0, lo:hi, :]
                x = xs[xi][:, lo:hi]
                if w_trans[wi]:
                    d = lax.dot_general(x, w, (((1,), (1,)), ((), ())), preferred_element_type=F32)
                else:
                    d = _dot(x, w)
                accs[gi] = d if accs[gi] is None else accs[gi] + d
        return accs

    k_max = max(x.shape[1] for x in xs)
    if not scr_of:
        finish(products(0, k_max, False))
        return

    def rounding_pass():
        accs = None
        for k0 in range(0, k_max, ROUND_SLAB):
            part = products(k0, min(k0 + ROUND_SLAB, k_max), True)
            accs = part if accs is None else [a if b is None else (b if a is None else a + b)
                                              for a, b in zip(accs, part)]
        finish(accs)

    if x_resident:
        rounding_pass()
        return

    pl.when(pl.program_id(1) == 0)(rounding_pass)

    @pl.when(pl.program_id(1) != 0)
    def _():
        finish(products(0, k_max, False))


def _matmul(name, layer, xs, ws, groups, epi, epi_args, n, out_dtype, tn, tm_cap, w_trans=None, row_scale=None,
            next_gain=None, x_resident=False):
    m = xs[0][0].shape[0]
    tm = _row_tile(m, tm_cap)
    w_trans = w_trans or (False,) * len(ws)
    if x_resident:
        grid, row_of, col_of = (m // tm, n // tn), (lambda a, b: a), (lambda a, b: b)
        held = dict(pipeline_mode=pl.Buffered(1))
    else:
        grid, row_of, col_of = (n // tn, m // tm), (lambda a, b: b), (lambda a, b: a)
        held = {}
    in_specs = []
    for _, kb, ki in xs:
        in_specs.append(pl.BlockSpec((tm, kb), lambda a, b, ki=ki: (row_of(a, b), ki), **held))
    w_blocks = []
    for (_, kb, ki), t in zip(ws, w_trans):
        if t:
            w_blocks.append((tn, kb))
            in_specs.append(pl.BlockSpec((1, tn, kb), lambda a, b, ki=ki: (layer, col_of(a, b), ki)))
        else:
            w_blocks.append((kb, tn))
            in_specs.append(pl.BlockSpec((1, kb, tn), lambda a, b, ki=ki: (layer, ki, col_of(a, b))))
    tile = pl.BlockSpec((tm, tn), lambda a, b: (row_of(a, b), col_of(a, b)))
    operands = [a for a, _, _ in xs] + [a for a, _, _ in ws] + list(epi_args)
    in_specs += [tile] * len(epi_args)
    n_epi = len(epi_args)
    if row_scale is not None:
        operands.append(row_scale)
        in_specs.append(pl.BlockSpec((tm, LANES), lambda a, b: (row_of(a, b), 0), **held))
        n_epi += 1
    out_specs, out_shape = tile, jax.ShapeDtypeStruct((m, n), out_dtype)
    if next_gain is not None:
        operands.append(next_gain.reshape(1, n))
        in_specs.append(pl.BlockSpec((1, tn), lambda a, b: (0, col_of(a, b))))
        out_specs = [tile, tile, pl.BlockSpec((tm, LANES), lambda a, b: (row_of(a, b), col_of(a, b)))]
        out_shape = [out_shape, jax.ShapeDtypeStruct((m, n), BF16),
                     jax.ShapeDtypeStruct((m, n // tn * LANES), F32)]
    w_cast = tuple(a.dtype != BF16 for a, _, _ in ws)
    kern = functools.partial(_mm_kernel, nx=len(xs), w_cast=w_cast, w_trans=w_trans, groups=groups, epi=epi,
                             n_epi=n_epi, next_norm=next_gain is not None, x_resident=x_resident)
    res = pl.pallas_call(
        kern,
        grid=grid,
        in_specs=in_specs,
        out_specs=out_specs,
        out_shape=out_shape,
        scratch_shapes=[pltpu.VMEM(blk, BF16) for blk, c in zip(w_blocks, w_cast) if c],
        compiler_params=_params("arbitrary", "arbitrary"),
        name=name,
    )(*operands)
    if next_gain is None:
        return res
    out, xg, ssq = res
    return out, xg, _row_scale(ssq, n)


def _epi_swiglu(a, b, rs):
    r = _lanes(rs, a.shape[1])
    return _silu(a * r) * (b * r)


def _epi_half_residual(acc, res):
    return res + 0.5 * acc


def _epi_residual(acc, res):
    return res + acc


def _epi_scaled(acc, rs):
    return acc * _lanes(rs, acc.shape[1])


def _epi_gated(a, c, res, rs):
    return res + jax.nn.sigmoid(a * _lanes(rs, a.shape[1])) * c


def _ffn(x, xg, rs, layer, next_gain, wg, wu, wd):
    _, d, f = wg.shape
    h = _matmul("ffn_gate_up", layer, [(xg, d, 0)], [(wg, d, 0), (wu, d, 0)], [[(0, 0)], [(0, 1)]], _epi_swiglu, [],
                f, BF16, tn=256, tm_cap=TALL_ROWS, row_scale=rs, x_resident=True)
    return _matmul("ffn_down", layer, [(h, f, 0)], [(wd, f, 0)], [[(0, 0)]], _epi_half_residual, [x],
                   d, F32, tn=512, tm_cap=432, next_gain=next_gain)


Q_SCALE = HEAD_DIM ** -0.5 * math.log2(math.e)


def _qk_prep_kernel(q_ref, k_ref, qg_ref, kg_ref, bd_ref, qo_ref, ko_ref):
    bd = bd_ref[...]

    def norm(x, g):
        ms = _dot01_rhs(x * x, bd, parts=2) * (1.0 / HEAD_DIM)
        return x * lax.rsqrt(ms + EPS) * g

    qg = qg_ref[...]
    kg = kg_ref[...]
    for c in range(Q_COLS // LANES):
        sl = slice(c * LANES, (c + 1) * LANES)
        qo_ref[:, sl] = (norm(q_ref[:, sl], qg) * Q_SCALE).astype(qo_ref.dtype)
    for c in range(K_COLS // LANES):
        sl = slice(c * LANES, (c + 1) * LANES)
        ko_ref[:, sl] = norm(k_ref[:, sl], kg)


def _qk_prep(h, q_g, k_g):
    m = h.shape[0]
    tm = _row_tile(m, 320)
    lane = jnp.arange(LANES)
    bd = (lane[:, None] // HEAD_DIM == lane[None, :] // HEAD_DIM).astype(BF16)
    qg2 = jnp.tile(q_g.astype(F32), LANES // HEAD_DIM).reshape(1, LANES)
    kg2 = jnp.tile(k_g.astype(F32), LANES // HEAD_DIM).reshape(1, LANES)
    return pl.pallas_call(
        _qk_prep_kernel,
        grid=(m // tm,),
        in_specs=[
            pl.BlockSpec((tm, Q_COLS), lambda i: (i, 0)),
            pl.BlockSpec((tm, K_COLS), lambda i: (i, Q_COLS // K_COLS)),
            pl.BlockSpec((1, LANES), lambda i: (0, 0)),
            pl.BlockSpec((1, LANES), lambda i: (0, 0)),
            pl.BlockSpec((LANES, LANES), lambda i: (0, 0)),
        ],
        out_specs=[pl.BlockSpec((tm, Q_COLS), lambda i: (i, 0)), pl.BlockSpec((tm, K_COLS), lambda i: (i, 0))],
        out_shape=[jax.ShapeDtypeStruct((m, Q_COLS), BF16), jax.ShapeDtypeStruct((m, K_COLS), F32)],
        compiler_params=_params("arbitrary"),
        name="qk_norm",
    )(h, h, qg2, kg2, bd)


def _lambda(lq1, lk1, lq2, lk2, lam0):
    s1 = jnp.sum(lq1[...] * lk1[...], axis=-1, keepdims=True)
    s2 = jnp.sum(lq2[...] * lk2[...], axis=-1, keepdims=True)
    return jnp.exp(s1) - jnp.exp(s2) + lam0


def _sub_layer_norm(o, g, lam0):
    ms = jnp.mean(o * o, axis=-1, keepdims=True)
    return (o * lax.rsqrt(ms + EPS) * g) * (1.0 - lam0)


SUM_ROWS = 16


def _attn_prompt_kernel(lq1, lk1, lq2, lk2, sg_ref, q_ref, k_ref, v_ref, o_ref, kb_sc, vt_sc, m_sc, acc_sc,
                        *, tq, nq, lam0):
    qi = pl.program_id(2)

    @pl.when(qi == 0)
    def _():
        kb_sc[...] = k_ref[...].astype(BF16)
        ones_rows = (lax.broadcasted_iota(jnp.int32, (SUM_ROWS, tq), 0) == 0).astype(BF16)
        for j in range(nq):
            vt_sc[j, :V_DIM, :] = v_ref[j * tq:(j + 1) * tq, :].T.astype(BF16)
            vt_sc[j, V_DIM:, :] = ones_rows

    q2 = q_ref[...]
    qs = jnp.concatenate([q2[:, :LANES], q2[:, LANES:]], axis=0)
    lane = lax.broadcasted_iota(jnp.int32, qs.shape, 1)
    zero = jnp.zeros_like(qs)
    qm = (jnp.where(lane < HEAD_DIM, qs, zero), jnp.where(lane >= HEAD_DIM, qs, zero))
    m_sc[...] = jnp.full(m_sc.shape, -jnp.inf, F32)
    acc_sc[...] = jnp.zeros(acc_sc.shape, F32)

    def block(kj, masked):
        start = pl.multiple_of(kj * tq, tq)
        kb = kb_sc[pl.ds(start, tq), :]
        vt = vt_sc[kj]
        sts = [lax.dot_general(kb, qm[m], (((1,), (1,)), ((), ())), preferred_element_type=F32) for m in range(2)]
        for m in range(2):
            st = sts[m]
            if masked:
                key = lax.broadcasted_iota(jnp.int32, st.shape, 0)
                qry = lax.broadcasted_iota(jnp.int32, st.shape, 1) % tq
                st = jnp.where(key <= qry, st, NEG)
            m_prev = m_sc[m]
            m_new = jnp.maximum(m_prev, jnp.max(st, axis=0, keepdims=True))
            p = jnp.exp2(st - m_new).astype(BF16)
            acc_sc[m] = jnp.exp2(m_prev - m_new) * acc_sc[m] + _dot(vt, p)
            m_sc[m] = m_new

    def body(kj, carry):
        block(kj, False)
        return carry

    lax.fori_loop(0, qi, body, 0)
    block(qi, True)

    lam = _lambda(lq1, lk1, lq2, lk2, lam0)
    o0, o1 = acc_sc[0], acc_sc[1]
    ot = o0[:V_DIM] / o0[V_DIM:V_DIM + 1] - lam * (o1[:V_DIM] / o1[V_DIM:V_DIM + 1])
    ms = jnp.mean(ot * ot, axis=0, keepdims=True)
    on = (ot * lax.rsqrt(ms + EPS)).T
    on = ((on * sg_ref[...]) * (1.0 - lam0)).astype(o_ref.dtype)
    o_ref[:, :LANES] = on[:tq]
    o_ref[:, LANES:] = on[tq:]


def _lam_specs():
    zeros = (0,) * 2
    return [pl.BlockSpec((1, HEAD_DIM), lambda *a: zeros) for _ in range(4)] + [pl.BlockSpec((1, V_DIM), lambda *a: zeros)]


def _attn_prompt(qn, kn, h, lam_vecs, subln_g, lam0, batch, seq):
    tq = next((t for t in (512, 256) if seq % t == 0), seq)
    nq = seq // tq
    kern = functools.partial(_attn_prompt_kernel, tq=tq, nq=nq, lam0=lam0)
    v_blk = (Q_COLS + K_COLS) // V_DIM
    return pl.pallas_call(
        kern,
        grid=(batch, KV_HEADS, nq),
        in_specs=_lam_specs() + [
            pl.BlockSpec((tq, 2 * LANES), lambda b, g, i: (b * nq + i, g)),
            pl.BlockSpec((seq, LANES), lambda b, g, i: (b, g)),
            pl.BlockSpec((seq, V_DIM), lambda b, g, i: (b, v_blk + g)),
        ],
        out_specs=pl.BlockSpec((tq, 2 * V_DIM), lambda b, g, i: (b * nq + i, g)),
        out_shape=jax.ShapeDtypeStruct((batch * seq, ATT_WIDTH), BF16),
        scratch_shapes=[pltpu.VMEM((seq, LANES), BF16), pltpu.VMEM((nq, V_DIM + SUM_ROWS, tq), BF16),
                        pltpu.VMEM((2, 1, 2 * tq), F32), pltpu.VMEM((2, V_DIM + SUM_ROWS, 2 * tq), F32)],
        compiler_params=_params("arbitrary", "arbitrary", "arbitrary"),
        name="attn_prompt",
    )(*lam_vecs, subln_g.reshape(1, V_DIM), qn, kn, h)


DECODE_ROWS = 2 * KV_HEADS * KV_REP


def _attn_decode_kernel(pt_ref, lq1, lk1, lq2, lk2, sg_ref, q_ref, kn_ref, vn_ref, *refs, pages, lam0):
    ck_refs, cv_refs = refs[:pages], refs[pages:2 * pages]
    o_ref, m_sc, l_sc, acc_sc = refs[2 * pages:]
    p = pl.program_id(1)
    page_rows = PAGE_SIZE * KV_HEADS

    @pl.when(p == 0)
    def _():
        m_sc[...] = jnp.full(m_sc.shape, -jnp.inf, F32)
        l_sc[...] = jnp.zeros(l_sc.shape, F32)
        acc_sc[...] = jnp.zeros(acc_sc.shape, F32)

    q = q_ref[0]
    qb = q.astype(BF16)
    row_head = (lax.broadcasted_iota(jnp.int32, (DECODE_ROWS, page_rows), 0) % (KV_HEADS * KV_REP)) // KV_REP
    col_head = lax.broadcasted_iota(jnp.int32, (DECODE_ROWS, page_rows), 1) % KV_HEADS
    same_head = row_head == col_head

    scores, values = [], []
    for j in range(pages):
        kb = ck_refs[j][0, 0].astype(BF16)
        s = lax.dot_general(qb, kb, (((1,), (1,)), ((), ())), preferred_element_type=F32)
        scores.append(jnp.where(same_head, s, NEG))
        values.append(cv_refs[j][0, 0].astype(BF16))
    m_prev = m_sc[...]
    m_new = m_prev
    for s in scores:
        m_new = jnp.maximum(m_new, jnp.max(s, axis=-1, keepdims=True))
    alpha = jnp.exp2(m_prev - m_new)
    l_new = alpha * l_sc[...]
    acc = alpha * acc_sc[...]
    for s, vb in zip(scores, values):
        pr = jnp.exp2(s - m_new)
        l_new = l_new + jnp.sum(pr, axis=-1, keepdims=True)
        acc = acc + _dot(pr.astype(BF16), vb)
    m_sc[...] = m_new
    l_sc[...] = l_new
    acc_sc[...] = acc

    @pl.when(p == pl.num_programs(1) - 1)
    def _():
        s = jnp.sum(q * kn_ref[0], axis=-1, keepdims=True)
        m_fin = jnp.maximum(m_new, s)
        a = jnp.exp2(m_new - m_fin)
        pr = jnp.exp2(s - m_fin)
        l_fin = a * l_new + pr
        o = (a * acc + pr * vn_ref[0]) / l_fin
        half = DECODE_ROWS // 2
        lam = _lambda(lq1, lk1, lq2, lk2, lam0)
        o_ref[0] = _sub_layer_norm(o[:half] - lam * o[half:], sg_ref[...], lam0)


def _attn_decode(qs, ks, vs, cache_k, cache_v, page_table, layer, lam_vecs, subln_g, lam0):
    db, n_pages = page_table.shape
    depth, n_phys = cache_k.shape[:2]
    pages = next(k for k in (8, 4, 2, 1) if n_pages % k == 0)
    page_rows = PAGE_SIZE * KV_HEADS
    ck = cache_k.reshape(depth, n_phys, page_rows, 2 * HEAD_DIM)
    cv = cache_v.reshape(depth, n_phys, page_rows, V_DIM)
    q5 = qs.reshape(db, KV_HEADS, KV_REP, 2, HEAD_DIM)
    z = jnp.zeros((db, KV_HEADS, KV_REP, HEAD_DIM), F32)
    qall = jnp.stack([jnp.concatenate([q5[:, :, :, 0], z], -1), jnp.concatenate([z, q5[:, :, :, 1]], -1)], axis=1)
    qall = qall.reshape(db, DECODE_ROWS, LANES)
    per_row = lambda t: jnp.broadcast_to(t.reshape(db, 1, KV_HEADS, 1, LANES),
                                         (db, 2, KV_HEADS, KV_REP, LANES)).reshape(db, DECODE_ROWS, LANES)
    kern = functools.partial(_attn_decode_kernel, pages=pages, lam0=lam0)
    row = lambda b, p, pt: (b, 0, 0)
    page = lambda j: (lambda b, p, pt: (layer, pt[b, p * pages + j], 0, 0))
    grid_spec = pltpu.PrefetchScalarGridSpec(
        num_scalar_prefetch=1,
        grid=(db, n_pages // pages),
        in_specs=_lam_specs() + [pl.BlockSpec((1, DECODE_ROWS, LANES), row) for _ in range(3)]
        + [pl.BlockSpec((1, 1, page_rows, 2 * HEAD_DIM), page(j)) for j in range(pages)]
        + [pl.BlockSpec((1, 1, page_rows, V_DIM), page(j)) for j in range(pages)],
        out_specs=pl.BlockSpec((1, DECODE_ROWS // 2, V_DIM), row),
        scratch_shapes=[pltpu.VMEM((DECODE_ROWS, 1), F32), pltpu.VMEM((DECODE_ROWS, 1), F32),
                        pltpu.VMEM((DECODE_ROWS, V_DIM), F32)],
    )
    out = pl.pallas_call(
        kern,
        grid_spec=grid_spec,
        out_shape=jax.ShapeDtypeStruct((db, DECODE_ROWS // 2, V_DIM), F32),
        compiler_params=_params("arbitrary", "arbitrary"),
        name="attn_decode",
    )(page_table, *lam_vecs, subln_g.reshape(1, V_DIM), qall, per_row(ks), per_row(vs),
      *([ck] * pages), *([cv] * pages))
    return out.reshape(db, ATT_WIDTH)


def _gate_norm(y, xs_c, z, d_e, g_e):
    y = (y + d_e * xs_c) * _silu(z)
    outs = []
    for g in range(SSM_GROUPS):
        yg = y[:, g * SSM_GROUP_W:(g + 1) * SSM_GROUP_W]
        ms = jnp.mean(yg * yg, axis=-1, keepdims=True)
        outs.append(yg * lax.rsqrt(ms + EPS) * g_e[:, g * SSM_GROUP_W:(g + 1) * SSM_GROUP_W])
    return outs


def _dt_raw(xg, rs, wdt_ref):
    w = wdt_ref[0].astype(BF16)
    w = jnp.concatenate([w, jnp.zeros((LANES - SSM_HEADS, w.shape[1]), BF16)], axis=0)
    return lax.dot_general(xg, w, (((1,), (1,)), ((), ())), preferred_element_type=F32) * rs


def _dt_weight_spec(layer):
    return pl.BlockSpec((1, SSM_HEADS, D_MODEL), lambda *a: (layer, DT_OFF // SSM_HEADS, 0))


def _ssd_prompt_kernel(z_ref, x_ref, b_ref, c_ref, xg_ref, rs_ref, wdt_ref, cwx, cwb, cwc, cbx, cbb, cbc, dtb_c, alog_c,
                       d_e, g_e, tril_ref, exp_ref, o_ref, s_ref, xpx, xpb, xpc, st_sc, y_sc):
    c = pl.program_id(1)
    q = SSD_CHUNK

    @pl.when(c == 0)
    def _():
        xpx[0:8, :] = jnp.zeros((8, xpx.shape[1]), F32)
        xpb[0:8, :] = jnp.zeros((8, xpb.shape[1]), F32)
        xpc[0:8, :] = jnp.zeros((8, xpc.shape[1]), F32)
        st_sc[...] = jnp.zeros(st_sc.shape, F32)

    def conv(src, pad, w, b):
        pad[8:8 + q, :] = src[...]
        y = b[...]
        for t in range(CONV_WIDTH):
            y = y + w[t:t + 1, :] * pad[5 + t:5 + t + q, :]
        tail = pad[q:q + 8, :]
        pad[0:8, :] = tail
        return _silu(y)

    xs_c = conv(x_ref, xpx, cwx, cbx)
    bc = conv(b_ref, xpb, cwb, cbb)
    cc = conv(c_ref, xpc, cwc, cbc)

    tril = tril_ref[...]
    dtr = _dt_raw(xg_ref[...], rs_ref[...], wdt_ref)
    dt_c = _softplus(dtr + dtb_c[...])
    acs_c = _dot01_lhs(tril, dt_c * (-jnp.exp(alog_c[...])))
    acs_t = acs_c.T
    dt_e = _dot01_rhs(dt_c, exp_ref[...])
    acs_e = _dot01_rhs(acs_c, exp_ref[...])
    a_last = acs_e[q - 1:q, :]
    xdt = xs_c * dt_e
    xdtd = (xdt * jnp.exp(a_last - acs_e)).astype(BF16)
    xdt = xdt.astype(BF16)
    ea = jnp.exp(acs_e)
    dec_end = jnp.exp(a_last)

    ii = lax.broadcasted_iota(jnp.int32, (q, q), 0)
    jj = lax.broadcasted_iota(jnp.int32, (q, q), 1)
    causal = jj <= ii
    lane_lo = jj < SSM_HEAD_DIM

    for g in range(SSM_GROUPS):
        gs = slice(g * SSM_GROUP_W, (g + 1) * SSM_GROUP_W)
        bg = bc[:, g * SSM_STATE:(g + 1) * SSM_STATE].astype(BF16)
        cg = cc[:, g * SSM_STATE:(g + 1) * SSM_STATE].astype(BF16)
        cb = lax.dot_general(cg, bg, (((1,), (1,)), ((), ())), preferred_element_type=F32)
        st = st_sc[g]
        y_off = _dot(cg, st.astype(BF16)) * ea[:, gs]
        pairs = []
        for pr in range(SSM_HPG // 2):
            col = g * SSM_GROUP_W + pr * LANES
            xpair = xdt[:, col:col + LANES]
            ys = []
            for hh in range(2):
                head = g * SSM_HPG + pr * 2 + hh
                seg = acs_c[:, head:head + 1] - acs_t[head:head + 1, :]
                lmat = jnp.where(causal, jnp.exp(seg), 0.0)
                ys.append(_dot((cb * lmat).astype(BF16), xpair))
            pairs.append(jnp.where(lane_lo, ys[0], ys[1]))
        y_sc[:, gs] = jnp.concatenate(pairs, axis=1) + y_off
        upd = lax.dot_general(bg, xdtd[:, gs], (((0,), (0,)), ((), ())), preferred_element_type=F32)
        st_sc[g] = st * dec_end[:, gs] + upd

    outs = _gate_norm(y_sc[...], xs_c, z_ref[...], d_e[...], g_e[...])
    for g in range(SSM_GROUPS):
        o_ref[:, g * SSM_GROUP_W:(g + 1) * SSM_GROUP_W] = outs[g].astype(o_ref.dtype)

    @pl.when(c == pl.num_programs(1) - 1)
    def _():
        for g in range(SSM_GROUPS):
            t = st_sc[g].T
            for r in range(SSM_HPG):
                s_ref[0, g * SSM_HPG + r] = t[r * SSM_HEAD_DIM:(r + 1) * SSM_HEAD_DIM, :]


def _ssm_consts(conv_w, conv_b, dt_bias, a_log, d_skip, ssm_g):
    rep = lambda v: jnp.repeat(v.astype(F32), SSM_HEAD_DIM).reshape(1, SSM_WIDTH)
    pad = lambda v: jnp.pad(v.astype(F32), (0, LANES - SSM_HEADS)).reshape(1, LANES)
    head_of_col = jnp.arange(SSM_WIDTH) // SSM_HEAD_DIM
    expand = (jnp.arange(LANES)[:, None] == head_of_col[None, :]).astype(BF16)
    tril = (jnp.arange(SSD_CHUNK)[None, :] <= jnp.arange(SSD_CHUNK)[:, None]).astype(BF16)
    return dict(conv_w=conv_w.astype(F32), conv_b=conv_b.astype(F32).reshape(1, CONV_CH),
                dtb_c=pad(dt_bias), alog_c=pad(a_log),
                d_e=rep(d_skip), g_e=ssm_g.astype(F32).reshape(1, SSM_WIDTH), tril=tril, expand=expand)


def _conv_param_specs(zmap):
    n_x, n_b = SSM_WIDTH, SSM_GROUPS * SSM_STATE
    cmap = lambda k: (lambda *a: (0, k))
    return [pl.BlockSpec((CONV_WIDTH, n_x), cmap(0)), pl.BlockSpec((CONV_WIDTH, n_b), cmap(2)),
            pl.BlockSpec((CONV_WIDTH, n_b), cmap(3)),
            pl.BlockSpec((1, n_x), cmap(0)), pl.BlockSpec((1, n_b), cmap(2)), pl.BlockSpec((1, n_b), cmap(3))]


def _ssd_prompt(h, xg, rs, w_dt, layer, sc, batch, seq):
    q = SSD_CHUNK
    nc = seq // q
    rb = lambda b, c: b * nc + c
    n_b = SSM_GROUPS * SSM_STATE
    zmap = lambda b, c: (0, 0)
    vec = lambda n: pl.BlockSpec((1, n), zmap)
    in_specs = [
        pl.BlockSpec((q, SSM_WIDTH), lambda b, c: (rb(b, c), Z_OFF // SSM_WIDTH)),
        pl.BlockSpec((q, SSM_WIDTH), lambda b, c: (rb(b, c), X_OFF // SSM_WIDTH)),
        pl.BlockSpec((q, n_b), lambda b, c: (rb(b, c), B_OFF // n_b)),
        pl.BlockSpec((q, n_b), lambda b, c: (rb(b, c), C_OFF // n_b)),
        pl.BlockSpec((q, D_MODEL), lambda b, c: (rb(b, c), 0)),
        pl.BlockSpec((q, LANES), lambda b, c: (rb(b, c), 0)),
        _dt_weight_spec(layer),
    ] + _conv_param_specs(zmap) + [
        vec(LANES), vec(LANES), vec(SSM_WIDTH), vec(SSM_WIDTH),
        pl.BlockSpec((q, q), zmap), pl.BlockSpec((LANES, SSM_WIDTH), zmap),
    ]
    out, state = pl.pallas_call(
        _ssd_prompt_kernel,
        grid=(batch, nc),
        in_specs=in_specs,
        out_specs=[pl.BlockSpec((q, SSM_WIDTH), lambda b, c: (rb(b, c), 0)),
                   pl.BlockSpec((1, SSM_HEADS, SSM_HEAD_DIM, SSM_STATE), lambda b, c: (b, 0, 0, 0))],
        out_shape=[jax.ShapeDtypeStruct((batch * seq, SSM_WIDTH), BF16),
                   jax.ShapeDtypeStruct((batch, SSM_HEADS, SSM_HEAD_DIM, SSM_STATE), F32)],
        scratch_shapes=[pltpu.VMEM((q + 8, SSM_WIDTH), F32), pltpu.VMEM((q + 8, n_b), F32),
                        pltpu.VMEM((q + 8, n_b), F32),
                        pltpu.VMEM((SSM_GROUPS, SSM_STATE, SSM_GROUP_W), F32), pltpu.VMEM((q, SSM_WIDTH), F32)],
        compiler_params=_params("arbitrary", "arbitrary"),
        name="ssd_prompt",
    )(h, h, h, h, xg, rs, w_dt, sc["conv_w"], sc["conv_w"], sc["conv_w"], sc["conv_b"], sc["conv_b"], sc["conv_b"],
      sc["dtb_c"], sc["alog_c"], sc["d_e"], sc["g_e"], sc["tril"], sc["expand"])
    return out, state


def _ssm_decode_pre_kernel(x_ref, b_ref, c_ref, xg_ref, rs_ref, wdt_ref, prev_ref, cwx, cwb, cwc, cbx, cbb, cbc, dtb_c, alog_c,
                           exp_ref, xs_o, b_o, c_o, xdt_o, dec_o):
    def conv(src, col, w, b):
        n = src.shape[1]
        y = b[...] + w[CONV_WIDTH - 1:CONV_WIDTH, :] * src[...]
        for t in range(CONV_WIDTH - 1):
            y = y + w[t:t + 1, :] * prev_ref[t, :, col:col + n]
        return _silu(y)

    xs_c = conv(x_ref, 0, cwx, cbx)
    xs_o[...] = xs_c
    b_o[...] = conv(b_ref, SSM_WIDTH, cwb, cbb)
    c_o[...] = conv(c_ref, SSM_WIDTH + SSM_GROUPS * SSM_STATE, cwc, cbc)
    dtr = _dt_raw(xg_ref[...], rs_ref[...], wdt_ref)[:x_ref.shape[0]]
    dt_c = _softplus(dtr + dtb_c[...])
    dec_o[...] = jnp.exp(dt_c * (-jnp.exp(alog_c[...])))
    xdt_o[...] = xs_c * _dot01_rhs(dt_c, exp_ref[...])


def _ssm_decode_state_kernel(s_ref, xdt_ref, dec_ref, b_ref, c_ref, so_ref, y_ref):
    for head in range(SSM_HEADS):
        g = head // SSM_HPG
        s = dec_ref[0, head] * s_ref[0, 0, head] + xdt_ref[0, head] * b_ref[0, g]
        so_ref[0, head] = s
        y_ref[0, head] = jnp.sum(s * c_ref[0, g], axis=-1, keepdims=True)


def _ssm_decode_post_kernel(y_ref, xs_ref, z_ref, d_e, g_e, o_ref):
    outs = _gate_norm(y_ref[...], xs_ref[...], z_ref[...], d_e[...], g_e[...])
    o_ref[...] = jnp.zeros(o_ref.shape, o_ref.dtype)
    db = y_ref.shape[0]
    for g in range(SSM_GROUPS):
        o_ref[0:db, g * SSM_GROUP_W:(g + 1) * SSM_GROUP_W] = outs[g]


def _ssm_decode(h, xg, rs, w_dt, sc, state_conv_l, state_ssm, layer, row0, db):
    n_b = SSM_GROUPS * SSM_STATE
    rblk = row0 // db
    zmap = lambda i: (0, 0)
    vec = lambda n: pl.BlockSpec((1, n), zmap)
    prev = jnp.transpose(state_conv_l, (1, 0, 2)).astype(F32)
    f = lambda n: jax.ShapeDtypeStruct((db, n), F32)
    xs_c, b_c, c_c, xdt, dec = pl.pallas_call(
        _ssm_decode_pre_kernel,
        grid=(1,),
        in_specs=[
            pl.BlockSpec((db, SSM_WIDTH), lambda i: (rblk, X_OFF // SSM_WIDTH)),
            pl.BlockSpec((db, n_b), lambda i: (rblk, B_OFF // n_b)),
            pl.BlockSpec((db, n_b), lambda i: (rblk, C_OFF // n_b)),
            pl.BlockSpec((PAD_ROWS, D_MODEL), lambda i: (row0 // PAD_ROWS, 0)),
            pl.BlockSpec((PAD_ROWS, LANES), lambda i: (row0 // PAD_ROWS, 0)),
            _dt_weight_spec(layer),
            pl.BlockSpec((CONV_WIDTH - 1, db, CONV_CH), lambda i: (0, 0, 0)),
        ] + _conv_param_specs(zmap) + [vec(LANES), vec(LANES),
                                       pl.BlockSpec((LANES, SSM_WIDTH), zmap)],
        out_specs=[pl.BlockSpec((db, SSM_WIDTH), zmap), pl.BlockSpec((db, n_b), zmap), pl.BlockSpec((db, n_b), zmap),
                   pl.BlockSpec((db, SSM_WIDTH), zmap), pl.BlockSpec((db, LANES), zmap)],
        out_shape=[f(SSM_WIDTH), f(n_b), f(n_b), f(SSM_WIDTH), f(LANES)],
        compiler_params=_params("arbitrary"),
        name="ssm_decode_pre",
    )(h, h, h, xg, rs, w_dt, prev, sc["conv_w"], sc["conv_w"], sc["conv_w"], sc["conv_b"], sc["conv_b"], sc["conv_b"],
      sc["dtb_c"], sc["alog_c"], sc["expand"])

    xdt4 = xdt.reshape(db, SSM_HEADS, SSM_HEAD_DIM, 1)
    dec4 = dec[:, :SSM_HEADS].reshape(db, SSM_HEADS, 1, 1)
    b4 = b_c.reshape(db, SSM_GROUPS, 1, SSM_STATE)
    c4 = c_c.reshape(db, SSM_GROUPS, 1, SSM_STATE)
    s_new, y4 = pl.pallas_call(
        _ssm_decode_state_kernel,
        grid=(db,),
        in_specs=[
            pl.BlockSpec((1, 1, SSM_HEADS, SSM_HEAD_DIM, SSM_STATE), lambda b: (layer, b, 0, 0, 0)),
            pl.BlockSpec((1, SSM_HEADS, SSM_HEAD_DIM, 1), lambda b: (b, 0, 0, 0)),
            pl.BlockSpec((1, SSM_HEADS, 1, 1), lambda b: (b, 0, 0, 0)),
            pl.BlockSpec((1, SSM_GROUPS, 1, SSM_STATE), lambda b: (b, 0, 0, 0)),
            pl.BlockSpec((1, SSM_GROUPS, 1, SSM_STATE), lambda b: (b, 0, 0, 0)),
        ],
        out_specs=[pl.BlockSpec((1, SSM_HEADS, SSM_HEAD_DIM, SSM_STATE), lambda b: (b, 0, 0, 0)),
                   pl.BlockSpec((1, SSM_HEADS, SSM_HEAD_DIM, 1), lambda b: (b, 0, 0, 0))],
        out_shape=[jax.ShapeDtypeStruct((db, SSM_HEADS, SSM_HEAD_DIM, SSM_STATE), F32),
                   jax.ShapeDtypeStruct((db, SSM_HEADS, SSM_HEAD_DIM, 1), F32)],
        compiler_params=_params("arbitrary"),
        name="ssm_decode_state",
    )(state_ssm, xdt4, dec4, b4, c4)

    y = y4.reshape(db, SSM_WIDTH)
    out = pl.pallas_call(
        _ssm_decode_post_kernel,
        grid=(1,),
        in_specs=[pl.BlockSpec((db, SSM_WIDTH), zmap), pl.BlockSpec((db, SSM_WIDTH), zmap),
                  pl.BlockSpec((db, SSM_WIDTH), lambda i: (rblk, Z_OFF // SSM_WIDTH)),
                  vec(SSM_WIDTH), vec(SSM_WIDTH)],
        out_specs=pl.BlockSpec((PAD_ROWS, SSM_WIDTH), zmap),
        out_shape=jax.ShapeDtypeStruct((PAD_ROWS, SSM_WIDTH), F32),
        compiler_params=_params("arbitrary"),
        name="ssm_decode_post",
    )(y, xs_c, h, sc["d_e"], sc["g_e"])
    return out, s_new


def _lambda_init(layer):
    return 0.8 - 0.6 * math.exp(-0.3 * layer)


def _token_mixing(x, xg, rs, layer, batch, seq, db, w, cache_k, cache_v, state_conv, state_ssm, page_table):
    n_prompt = batch * seq
    lam0 = _lambda_init(layer)
    w_dt = w["w_in_t"]
    h = _matmul("in_proj", layer, [(xg, D_MODEL, 0)], [(w_dt, D_MODEL, 0)], [[(0, 0)]], _epi_scaled, [],
                DT_OFF, F32, tn=512, tm_cap=1024, w_trans=(True,), row_scale=rs)
    qn, kn = _qk_prep(h, w["q_norm"], w["k_norm"])
    lam_vecs = [w[k].astype(F32).reshape(1, HEAD_DIM) for k in ("lambda_q1", "lambda_k1", "lambda_q2", "lambda_k2")]
    dec = slice(n_prompt, n_prompt + db)
    v_off = Q_COLS + K_COLS

    att = _attn_prompt(qn, kn, h, lam_vecs, w["attn_subln"], lam0, batch, seq)
    sc = _ssm_consts(w["conv_w"], w["conv_b"], w["dt_bias"], w["a_log"], w["d_skip"], w["ssm_norm"])
    ssm, ssm_state_p = _ssd_prompt(h, xg, rs, w_dt, layer, sc, batch, seq)

    qs = qn[dec].astype(F32)
    ks = kn[dec]
    vs = h[dec, v_off:v_off + V_COLS]
    att_s = _attn_decode(qs, ks, vs, cache_k, cache_v, page_table, layer, lam_vecs, w["attn_subln"], lam0)
    att = jnp.concatenate([att, jnp.pad(att_s, ((0, PAD_ROWS - db), (0, 0))).astype(BF16)], axis=0)
    ssm_s, ssm_state_s = _ssm_decode(h, xg, rs, w_dt, sc, state_conv[layer], state_ssm, layer, n_prompt, db)
    ssm = jnp.concatenate([ssm, ssm_s.astype(BF16)], axis=0)

    half = ATT_WIDTH
    x, xg, rs = _matmul("out_proj", layer, [(att, half, 0), (ssm, half, 0)],
                        [(w["w_out"], half, 0), (w["w_out"], half, 1)], [[(0, 0), (1, 1)]], _epi_residual, [x],
                        D_MODEL, F32, tn=512, tm_cap=1024, next_gain=w["ffn2_norm"])

    k_p = kn[:n_prompt].reshape(batch, seq, KV_HEADS, 2 * HEAD_DIM)
    v_p = h[:n_prompt, v_off:v_off + V_COLS].reshape(batch, seq, KV_HEADS, V_DIM)
    conv_p = jnp.stack([h[(b + 1) * seq - (CONV_WIDTH - 1):(b + 1) * seq, X_OFF:X_OFF + CONV_CH] for b in range(batch)])
    k_s = ks.reshape(db, 1, KV_HEADS, 2 * HEAD_DIM)
    v_s = vs.reshape(db, 1, KV_HEADS, V_DIM)
    conv_s = jnp.concatenate([state_conv[layer][:, 1:].astype(F32), h[dec, X_OFF:X_OFF + CONV_CH][:, None, :]], axis=1)
    return x, xg, rs, (k_p, v_p, conv_p, ssm_state_p, k_s, v_s, conv_s, ssm_state_s)


def _ple(x, xg, rs, layer, next_gain, p, w_gate, w_proj):
    return _matmul("ple", layer, [(xg, D_MODEL, 0), (p, PLE_DIM, 0)], [(w_gate, D_MODEL, 0), (w_proj, PLE_DIM, 0)],
                   [[(0, 0)], [(1, 1)]], _epi_gated, [x], D_MODEL, F32, tn=512, tm_cap=1024, row_scale=rs,
                   next_gain=next_gain)


def kernel(x_prompt, x_sample, cache_k, cache_v, state_conv, state_ssm, page_table, p_prompt, p_sample, ffn1_norm, ffn1_w_gate, ffn1_w_up, ffn1_w_down, mix_norm, w_in, q_norm, k_norm, lambda_q1, lambda_k1, lambda_q2, lambda_k2, attn_subln, conv_w, conv_b, dt_bias, a_log, d_skip, ssm_norm, w_out, ffn2_norm, ffn2_w_gate, ffn2_w_up, ffn2_w_down, ple_norm, ple_w_gate, ple_w_proj):
    batch, seq, d = x_prompt.shape
    db = x_sample.shape[0]
    depth = w_in.shape[0]
    n_prompt = batch * seq
    x = jnp.concatenate([x_prompt.reshape(n_prompt, d), x_sample.reshape(db, d),
                         jnp.zeros((PAD_ROWS - db, d), F32)], axis=0)
    p_all = jnp.concatenate([p_prompt.reshape(depth, n_prompt, PLE_DIM), p_sample.reshape(depth, db, PLE_DIM),
                             jnp.zeros((depth, PAD_ROWS - db, PLE_DIM), F32)], axis=1).astype(BF16)
    wd1 = ffn1_w_down.astype(BF16)
    wd2 = ffn2_w_down.astype(BF16)
    w_in_t = jnp.swapaxes(w_in, 1, 2)
    outs = []
    xg, rs = _norm_factors(x, ffn1_norm[0])
    for i in range(depth):
        x, xg, rs = _ffn(x, xg, rs, i, mix_norm[i], ffn1_w_gate, ffn1_w_up, wd1)
        w = dict(ffn2_norm=ffn2_norm[i], w_in_t=w_in_t,
                 q_norm=q_norm[i], k_norm=k_norm[i], lambda_q1=lambda_q1[i], lambda_k1=lambda_k1[i],
                 lambda_q2=lambda_q2[i], lambda_k2=lambda_k2[i], attn_subln=attn_subln[i], conv_w=conv_w[i],
                 conv_b=conv_b[i], dt_bias=dt_bias[i], a_log=a_log[i], d_skip=d_skip[i], ssm_norm=ssm_norm[i],
                 w_out=w_out)
        x, xg, rs, o = _token_mixing(x, xg, rs, i, batch, seq, db, w, cache_k, cache_v, state_conv, state_ssm,
                                     page_table)
        outs.append(o)
        x, xg, rs = _ffn(x, xg, rs, i, ple_norm[i], ffn2_w_gate, ffn2_w_up, wd2)
        if i + 1 < depth:
            x, xg, rs = _ple(x, xg, rs, i, ffn1_norm[i + 1], p_all[i], ple_w_gate, ple_w_proj)
        else:
            x = _ple(x, xg, rs, i, None, p_all[i], ple_w_gate, ple_w_proj)
    stacked = [jnp.stack([o[k] for o in outs]) for k in range(8)]
    y_prompt = x[:n_prompt].reshape(batch, seq, d)
    y_sample = x[n_prompt:n_prompt + db].reshape(db, 1, d)
    return (y_prompt, y_sample, *stacked)
```

```python
import functools
import math

import jax
import jax.numpy as jnp
from jax import lax
from jax.experimental import pallas as pl
from jax.experimental.pallas import tpu as pltpu

F32 = jnp.float32
BF16 = jnp.bfloat16

D_MODEL = 4096
PAGE_SIZE = 128
HEAD_DIM = 64
KV_HEADS = 8
KV_REP = 2
V_DIM = 128
ATT_WIDTH = KV_HEADS * KV_REP * V_DIM
Q_COLS = KV_HEADS * KV_REP * 2 * HEAD_DIM
K_COLS = KV_HEADS * 2 * HEAD_DIM
V_COLS = KV_HEADS * V_DIM
SSM_WIDTH = 2048
SSM_HEAD_DIM = 64
SSM_HEADS = 32
SSM_GROUPS = 8
SSM_HPG = 4
SSM_STATE = 128
SSM_GROUP_W = SSM_WIDTH // SSM_GROUPS
CONV_WIDTH = 4
CONV_CH = SSM_WIDTH + 2 * SSM_GROUPS * SSM_STATE
SSD_CHUNK = 128
D_FF = 11008
PLE_DIM = 256
EPS = 1e-6
NEG = -1e30
Z_OFF = Q_COLS + K_COLS + V_COLS
X_OFF = Z_OFF + SSM_WIDTH
B_OFF = X_OFF + SSM_WIDTH
C_OFF = B_OFF + SSM_GROUPS * SSM_STATE
DT_OFF = C_OFF + SSM_GROUPS * SSM_STATE
IN_COLS = DT_OFF + SSM_HEADS
PAD_ROWS = 16
LANES = 128
VMEM_LIMIT_BYTES = 56 * 1024 * 1024


def _params(*sem):
    return pltpu.CompilerParams(dimension_semantics=sem, vmem_limit_bytes=VMEM_LIMIT_BYTES)


def _row_tile(m, cap):
    best = 16
    for t in range(16, cap + 1, 16):
        if m % t == 0:
            best = t
    return best


def _split3(x):
    h = x.astype(BF16)
    r = x - h.astype(F32)
    m = r.astype(BF16)
    l = (r - m.astype(F32)).astype(BF16)
    return h, m, l


def _dot(a, b):
    return jnp.dot(a, b, preferred_element_type=F32)


def _dot01_rhs(x, e, parts=3):
    acc = None
    for part in _split3(x)[:parts]:
        d = _dot(part, e)
        acc = d if acc is None else acc + d
    return acc


def _dot01_lhs(e, x):
    h, m, l = _split3(x)
    return _dot(e, h) + _dot(e, m) + _dot(e, l)


def _silu(x):
    return x * jax.nn.sigmoid(x)


def _softplus(x):
    return jnp.maximum(x, 0.0) + jnp.log1p(jnp.exp(-jnp.abs(x)))


def _lanes(rs, n):
    return jnp.concatenate([rs] * (n // LANES), axis=1)


def _norm_factors_kernel(x_ref, g_ref, xg_ref, rs_ref):
    x = x_ref[...]
    ms = jnp.mean(x * x, axis=-1, keepdims=True)
    xg_ref[...] = (x * g_ref[...]).astype(xg_ref.dtype)
    rs_ref[...] = jnp.broadcast_to(lax.rsqrt(ms + EPS), rs_ref.shape)


def _norm_factors(x, g):
    m, d = x.shape
    tm = _row_tile(m, 320)
    return pl.pallas_call(
        _norm_factors_kernel,
        grid=(m // tm,),
        in_specs=[pl.BlockSpec((tm, d), lambda i: (i, 0)), pl.BlockSpec((1, d), lambda i: (0, 0))],
        out_specs=[pl.BlockSpec((tm, d), lambda i: (i, 0)), pl.BlockSpec((tm, LANES), lambda i: (i, 0))],
        out_shape=[jax.ShapeDtypeStruct((m, d), BF16), jax.ShapeDtypeStruct((m, LANES), F32)],
        compiler_params=_params("arbitrary"),
        name="norm_factors",
    )(x, g.reshape(1, d))


def _row_scale_kernel(s_ref, o_ref, *, width):
    s = s_ref[...]
    tot = s[:, :LANES]
    for c in range(1, s.shape[1] // LANES):
        tot = tot + s[:, c * LANES:(c + 1) * LANES]
    ms = jnp.sum(tot, axis=-1, keepdims=True) * (1.0 / width)
    o_ref[...] = jnp.broadcast_to(lax.rsqrt(ms + EPS), o_ref.shape)


def _row_scale(ssq, width):
    m, n = ssq.shape
    tm = _row_tile(m, 1024)
    return pl.pallas_call(
        functools.partial(_row_scale_kernel, width=width),
        grid=(m // tm,),
        in_specs=[pl.BlockSpec((tm, n), lambda i: (i, 0))],
        out_specs=pl.BlockSpec((tm, LANES), lambda i: (i, 0)),
        out_shape=jax.ShapeDtypeStruct((m, LANES), F32),
        compiler_params=_params("arbitrary"),
        name="row_scale",
    )(ssq)


ROUND_SLAB = 1024
TALL_ROWS = 2736


def _mm_kernel(*refs, nx, w_cast, w_trans, groups, epi, n_epi, next_norm, x_resident, side_round):
    nw = len(w_cast)
    xs, ws = refs[:nx], refs[nx:nx + nw]
    pos = nx + nw
    es = refs[pos:pos + n_epi]
    pos += n_epi
    if next_norm:
        gn_ref = refs[pos]
        pos += 1
    if side_round:
        side_in = refs[pos]
        pos += 1
    o_ref = refs[pos]
    pos += 1
    if next_norm:
        xg_ref, ssq_ref = refs[pos:pos + 2]
        pos += 2
    if side_round:
        side_out = refs[pos]
        pos += 1
    scrs = refs[pos:]
    scr_of = {}
    for wi, c in enumerate(w_cast):
        if c:
            scr_of[wi] = scrs[len(scr_of)]

    def finish(accs):
        y = epi(*accs, *[e[...] for e in es])
        o_ref[...] = y.astype(o_ref.dtype)
        if next_norm:
            xg_ref[...] = (y * gn_ref[...]).astype(xg_ref.dtype)
            sq = y * y
            part = sq[:, :LANES]
            for c in range(1, sq.shape[1] // LANES):
                part = part + sq[:, c * LANES:(c + 1) * LANES]
            ssq_ref[...] = part

    def products(k0, k1, round_weights):
        accs = [None] * len(groups)
        for gi, grp in enumerate(groups):
            for xi, wi in grp:
                kb = xs[xi].shape[1]
                lo, hi = min(k0, kb), min(k1, kb)
                if lo == hi:
                    continue
                if wi in scr_of:
                    scr = scr_of[wi]
                    if w_trans[wi]:
                        if round_weights:
                            scr[:, lo:hi] = ws[wi][0, :, lo:hi].astype(BF16)
                        w = scr[:, lo:hi]
                    else:
                        if round_weights:
                            scr[lo:hi, :] = ws[wi][0, lo:hi, :].astype(BF16)
                        w = scr[lo:hi, :]
                else:
                    w = ws[wi][0, :, lo:hi] if w_trans[wi] else ws[wi][0, lo:hi, :]
                x = xs[xi][:, lo:hi]
                if w_trans[wi]:
                    d = lax.dot_general(x, w, (((1,), (1,)), ((), ())), preferred_element_type=F32)
                else:
                    d = _dot(x, w)
                accs[gi] = d if accs[gi] is None else accs[gi] + d
        return accs

    k_max = max(x.shape[1] for x in xs)
    if not scr_of:
        finish(products(0, k_max, False))
        return

    def rounding_pass():
        if side_round:
            side_out[...] = side_in[0].astype(BF16)
        accs = None
        for k0 in range(0, k_max, ROUND_SLAB):
            part = products(k0, min(k0 + ROUND_SLAB, k_max), True)
            accs = part if accs is None else [a if b is None else (b if a is None else a + b)
                                              for a, b in zip(accs, part)]
        finish(accs)

    if x_resident:
        rounding_pass()
        return

    pl.when(pl.program_id(1) == 0)(rounding_pass)

    @pl.when(pl.program_id(1) != 0)
    def _():
        finish(products(0, k_max, False))


def _matmul(name, layer, xs, ws, groups, epi, epi_args, n, out_dtype, tn, tm_cap, w_trans=None, row_scale=None,
            next_gain=None, x_resident=False, side_round=None):
    m = xs[0][0].shape[0]
    tm = _row_tile(m, tm_cap)
    w_trans = w_trans or (False,) * len(ws)
    if x_resident:
        grid, row_of, col_of = (m // tm, n // tn), (lambda a, b: a), (lambda a, b: b)
        held = dict(pipeline_mode=pl.Buffered(1))
    else:
        grid, row_of, col_of = (n // tn, m // tm), (lambda a, b: b), (lambda a, b: a)
        held = {}
    in_specs = []
    for _, kb, ki in xs:
        in_specs.append(pl.BlockSpec((tm, kb), lambda a, b, ki=ki: (row_of(a, b), ki), **held))
    w_blocks = []
    for (_, kb, ki), t in zip(ws, w_trans):
        if t:
            w_blocks.append((tn, kb))
            in_specs.append(pl.BlockSpec((1, tn, kb), lambda a, b, ki=ki: (layer, col_of(a, b), ki)))
        else:
            w_blocks.append((kb, tn))
            in_specs.append(pl.BlockSpec((1, kb, tn), lambda a, b, ki=ki: (layer, ki, col_of(a, b))))
    tile = pl.BlockSpec((tm, tn), lambda a, b: (row_of(a, b), col_of(a, b)))
    operands = [a for a, _, _ in xs] + [a for a, _, _ in ws] + list(epi_args)
    in_specs += [tile] * len(epi_args)
    n_epi = len(epi_args)
    if row_scale is not None:
        operands.append(row_scale)
        in_specs.append(pl.BlockSpec((tm, LANES), lambda a, b: (row_of(a, b), 0), **held))
        n_epi += 1
    out_specs, out_shape = tile, jax.ShapeDtypeStruct((m, n), out_dtype)
    if next_gain is not None:
        operands.append(next_gain.reshape(1, n))
        in_specs.append(pl.BlockSpec((1, tn), lambda a, b: (0, col_of(a, b))))
        out_specs = [tile, tile, pl.BlockSpec((tm, LANES), lambda a, b: (row_of(a, b), col_of(a, b)))]
        out_shape = [out_shape, jax.ShapeDtypeStruct((m, n), BF16),
                     jax.ShapeDtypeStruct((m, n // tn * LANES), F32)]
    w_cast = tuple(a.dtype != BF16 for a, _, _ in ws)
    if side_round is not None:
        assert x_resident and next_gain is None and all(w_cast)
        parts = min(2, m // tm)
        _, sr, sc = side_round.shape
        srb = sr // (n // tn) // parts
        spare = sr // srb
        operands.append(side_round)
        in_specs.append(pl.BlockSpec((1, srb, sc), lambda a, b: (layer, parts * b + jnp.minimum(a, parts - 1), 0)))
        out_specs = [tile, pl.BlockSpec((srb, sc), lambda a, b: (jnp.where(a < parts, parts * b + a, spare), 0))]
        out_shape = [out_shape, jax.ShapeDtypeStruct((sr + srb, sc), BF16)]
    kern = functools.partial(_mm_kernel, nx=len(xs), w_cast=w_cast, w_trans=w_trans, groups=groups, epi=epi,
                             n_epi=n_epi, next_norm=next_gain is not None, x_resident=x_resident,
                             side_round=side_round is not None)
    res = pl.pallas_call(
        kern,
        grid=grid,
        in_specs=in_specs,
        out_specs=out_specs,
        out_shape=out_shape,
        scratch_shapes=[pltpu.VMEM(blk, BF16) for blk, c in zip(w_blocks, w_cast) if c],
        compiler_params=_params("arbitrary", "arbitrary"),
        name=name,
    )(*operands)
    if next_gain is None:
        return res
    out, xg, ssq = res
    return out, xg, _row_scale(ssq, n)


def _epi_swiglu(a, b, rs):
    r = _lanes(rs, a.shape[1])
    return _silu(a * r) * (b * r)


def _epi_half_residual(acc, res):
    return res + 0.5 * acc


def _epi_residual(acc, res):
    return res + acc


def _epi_scaled(acc, rs):
    return acc * _lanes(rs, acc.shape[1])


def _epi_gated(a, c, res, rs):
    return res + jax.nn.sigmoid(a * _lanes(rs, a.shape[1])) * c


def _ffn(x, xg, rs, layer, next_gain, wg, wu, wd):
    _, d, f = wg.shape
    h, wd_b = _matmul("ffn_gate_up", layer, [(xg, d, 0)], [(wg, d, 0), (wu, d, 0)], [[(0, 0)], [(0, 1)]], _epi_swiglu,
                      [], f, BF16, tn=256, tm_cap=TALL_ROWS, row_scale=rs, x_resident=True, side_round=wd)
    return _matmul("ffn_down", 0, [(h, f, 0)], [(wd_b[None], f, 0)], [[(0, 0)]], _epi_half_residual, [x],
                   d, F32, tn=512, tm_cap=432, next_gain=next_gain)


Q_SCALE = HEAD_DIM ** -0.5 * math.log2(math.e)


def _qk_prep_kernel(q_ref, k_ref, qg_ref, kg_ref, bd_ref, qo_ref, ko_ref):
    bd = bd_ref[...]

    def norm(x, g):
        ms = _dot01_rhs(x * x, bd, parts=2) * (1.0 / HEAD_DIM)
        return x * lax.rsqrt(ms + EPS) * g

    qg = qg_ref[...]
    kg = kg_ref[...]
    for c in range(Q_COLS // LANES):
        sl = slice(c * LANES, (c + 1) * LANES)
        qo_ref[:, sl] = (norm(q_ref[:, sl], qg) * Q_SCALE).astype(qo_ref.dtype)
    for c in range(K_COLS // LANES):
        sl = slice(c * LANES, (c + 1) * LANES)
        ko_ref[:, sl] = norm(k_ref[:, sl], kg)


def _qk_prep(h, q_g, k_g):
    m = h.shape[0]
    tm = _row_tile(m, 320)
    lane = jnp.arange(LANES)
    bd = (lane[:, None] // HEAD_DIM == lane[None, :] // HEAD_DIM).astype(BF16)
    qg2 = jnp.tile(q_g.astype(F32), LANES // HEAD_DIM).reshape(1, LANES)
    kg2 = jnp.tile(k_g.astype(F32), LANES // HEAD_DIM).reshape(1, LANES)
    return pl.pallas_call(
        _qk_prep_kernel,
        grid=(m // tm,),
        in_specs=[
            pl.BlockSpec((tm, Q_COLS), lambda i: (i, 0)),
            pl.BlockSpec((tm, K_COLS), lambda i: (i, Q_COLS // K_COLS)),
            pl.BlockSpec((1, LANES), lambda i: (0, 0)),
            pl.BlockSpec((1, LANES), lambda i: (0, 0)),
            pl.BlockSpec((LANES, LANES), lambda i: (0, 0)),
        ],
        out_specs=[pl.BlockSpec((tm, Q_COLS), lambda i: (i, 0)), pl.BlockSpec((tm, K_COLS), lambda i: (i, 0))],
        out_shape=[jax.ShapeDtypeStruct((m, Q_COLS), BF16), jax.ShapeDtypeStruct((m, K_COLS), F32)],
        compiler_params=_params("arbitrary"),
        name="qk_norm",
    )(h, h, qg2, kg2, bd)


def _lambda(lq1, lk1, lq2, lk2, lam0):
    s1 = jnp.sum(lq1[...] * lk1[...], axis=-1, keepdims=True)
    s2 = jnp.sum(lq2[...] * lk2[...], axis=-1, keepdims=True)
    return jnp.exp(s1) - jnp.exp(s2) + lam0


def _sub_layer_norm(o, g, lam0):
    ms = jnp.mean(o * o, axis=-1, keepdims=True)
    return (o * lax.rsqrt(ms + EPS) * g) * (1.0 - lam0)


SUM_ROWS = 16


def _attn_prompt_kernel(lq1, lk1, lq2, lk2, sg_ref, q_ref, k_ref, v_ref, tail_ref, o_ref, kb_sc, vt_sc, m_sc, acc_sc,
                        *, tq, nq, lam0):
    qi = pl.program_id(2)

    @pl.when(qi == 0)
    def _():
        kb_sc[...] = k_ref[...].astype(BF16)
        ones_rows = (lax.broadcasted_iota(jnp.int32, (SUM_ROWS, tq), 0) == 0).astype(BF16)
        for j in range(nq):
            vt_sc[j, :V_DIM, :] = v_ref[j * tq:(j + 1) * tq, :].T.astype(BF16)
            vt_sc[j, V_DIM:, :] = ones_rows

    q2 = q_ref[...]
    qs = jnp.concatenate([q2[:, :LANES], q2[:, LANES:]], axis=0)
    lane = lax.broadcasted_iota(jnp.int32, qs.shape, 1)
    zero = jnp.zeros_like(qs)
    qm = (jnp.where(lane < HEAD_DIM, qs, zero), jnp.where(lane >= HEAD_DIM, qs, zero))
    m_sc[...] = jnp.full(m_sc.shape, -jnp.inf, F32)
    acc_sc[...] = jnp.zeros(acc_sc.shape, F32)

    def block(kj, masked):
        kb = kb_sc[kj * tq:(kj + 1) * tq, :]
        vt = vt_sc[kj]
        sts = [lax.dot_general(kb, qm[m], (((1,), (1,)), ((), ())), preferred_element_type=F32) for m in range(2)]
        for m in range(2):
            st = sts[m]
            if masked:
                key = lax.broadcasted_iota(jnp.int32, st.shape, 0)
                qry = lax.broadcasted_iota(jnp.int32, st.shape, 1) % tq
                st = jnp.where(key <= qry, st, NEG)
            m_prev = m_sc[m]
            m_new = jnp.maximum(m_prev, jnp.max(st, axis=0, keepdims=True))
            p = jnp.exp2(st - m_new).astype(BF16)
            acc_sc[m] = jnp.exp2(m_prev - m_new) * acc_sc[m] + _dot(vt, p)
            m_sc[m] = m_new

    for q in range(nq):
        @pl.when(qi == q)
        def _(q=q):
            for kj in range(q):
                block(kj, False)
            block(q, True)

    lam = _lambda(lq1, lk1, lq2, lk2, lam0)
    o0, o1 = acc_sc[0], acc_sc[1]
    ot = o0[:V_DIM] / o0[V_DIM:V_DIM + 1] - lam * (o1[:V_DIM] / o1[V_DIM:V_DIM + 1])
    ms = jnp.mean(ot * ot, axis=0, keepdims=True)
    on = (ot * lax.rsqrt(ms + EPS)).T
    on = ((on * sg_ref[...]) * (1.0 - lam0)).astype(o_ref.dtype)
    o_ref[:, :LANES] = on[:tq]
    o_ref[:, LANES:] = on[tq:]


def _lam_specs():
    zeros = (0,) * 2
    return [pl.BlockSpec((1, HEAD_DIM), lambda *a: zeros) for _ in range(4)] + [pl.BlockSpec((1, V_DIM), lambda *a: zeros)]


def _attn_prompt(qn, kn, h, lam_vecs, subln_g, lam0, batch, seq, tail):
    tq = next((t for t in (512, 256) if seq % t == 0), seq)
    nq = seq // tq
    kern = functools.partial(_attn_prompt_kernel, tq=tq, nq=nq, lam0=lam0)
    v_blk = (Q_COLS + K_COLS) // V_DIM
    return pl.pallas_call(
        kern,
        grid=(batch, KV_HEADS, nq),
        in_specs=_lam_specs() + [
            pl.BlockSpec((tq, 2 * LANES), lambda b, g, i: (b * nq + i, g)),
            pl.BlockSpec((seq, LANES), lambda b, g, i: (b, g)),
            pl.BlockSpec((seq, V_DIM), lambda b, g, i: (b, v_blk + g)),
            pl.BlockSpec(memory_space=pl.ANY),
        ],
        out_specs=pl.BlockSpec((tq, 2 * V_DIM), lambda b, g, i: (b * nq + i, g)),
        out_shape=jax.ShapeDtypeStruct(tail.shape, BF16),
        input_output_aliases={8: 0},
        scratch_shapes=[pltpu.VMEM((seq, LANES), BF16), pltpu.VMEM((nq, V_DIM + SUM_ROWS, tq), BF16),
                        pltpu.VMEM((2, 1, 2 * tq), F32), pltpu.VMEM((2, V_DIM + SUM_ROWS, 2 * tq), F32)],
        compiler_params=_params("arbitrary", "arbitrary", "arbitrary"),
        name="attn_prompt",
    )(*lam_vecs, subln_g.reshape(1, V_DIM), qn, kn, h, tail)


DECODE_ROWS = 2 * KV_HEADS * KV_REP


def _attn_decode_kernel(pt_ref, lq1, lk1, lq2, lk2, sg_ref, q_ref, kn_ref, vn_ref, *refs, pages, lam0):
    ck_refs, cv_refs = refs[:pages], refs[pages:2 * pages]
    o_ref, m_sc, l_sc, acc_sc = refs[2 * pages:]
    p = pl.program_id(1)
    page_rows = PAGE_SIZE * KV_HEADS

    @pl.when(p == 0)
    def _():
        m_sc[...] = jnp.full(m_sc.shape, -jnp.inf, F32)
        l_sc[...] = jnp.zeros(l_sc.shape, F32)
        acc_sc[...] = jnp.zeros(acc_sc.shape, F32)

    q = q_ref[0]
    qb = q.astype(BF16)
    row_head = (lax.broadcasted_iota(jnp.int32, (DECODE_ROWS, page_rows), 0) % (KV_HEADS * KV_REP)) // KV_REP
    col_head = lax.broadcasted_iota(jnp.int32, (DECODE_ROWS, page_rows), 1) % KV_HEADS
    same_head = row_head == col_head

    scores, values = [], []
    for j in range(pages):
        kb = ck_refs[j][0, 0].astype(BF16)
        s = lax.dot_general(qb, kb, (((1,), (1,)), ((), ())), preferred_element_type=F32)
        scores.append(jnp.where(same_head, s, NEG))
        values.append(cv_refs[j][0, 0].astype(BF16))
    m_prev = m_sc[...]
    m_new = m_prev
    for s in scores:
        m_new = jnp.maximum(m_new, jnp.max(s, axis=-1, keepdims=True))
    alpha = jnp.exp2(m_prev - m_new)
    l_new = alpha * l_sc[...]
    acc = alpha * acc_sc[...]
    for s, vb in zip(scores, values):
        pr = jnp.exp2(s - m_new)
        l_new = l_new + jnp.sum(pr, axis=-1, keepdims=True)
        acc = acc + _dot(pr.astype(BF16), vb)
    m_sc[...] = m_new
    l_sc[...] = l_new
    acc_sc[...] = acc

    @pl.when(p == pl.num_programs(1) - 1)
    def _():
        s = jnp.sum(q * kn_ref[0], axis=-1, keepdims=True)
        m_fin = jnp.maximum(m_new, s)
        a = jnp.exp2(m_new - m_fin)
        pr = jnp.exp2(s - m_fin)
        l_fin = a * l_new + pr
        o = (a * acc + pr * vn_ref[0]) / l_fin
        half = DECODE_ROWS // 2
        lam = _lambda(lq1, lk1, lq2, lk2, lam0)
        o_ref[0] = _sub_layer_norm(o[:half] - lam * o[half:], sg_ref[...], lam0)


def _attn_decode(qs, ks, vs, cache_k, cache_v, page_table, layer, lam_vecs, subln_g, lam0):
    db, n_pages = page_table.shape
    depth, n_phys = cache_k.shape[:2]
    pages = next(k for k in (8, 4, 2, 1) if n_pages % k == 0)
    page_rows = PAGE_SIZE * KV_HEADS
    ck = cache_k.reshape(depth, n_phys, page_rows, 2 * HEAD_DIM)
    cv = cache_v.reshape(depth, n_phys, page_rows, V_DIM)
    q5 = qs.reshape(db, KV_HEADS, KV_REP, 2, HEAD_DIM)
    z = jnp.zeros((db, KV_HEADS, KV_REP, HEAD_DIM), F32)
    qall = jnp.stack([jnp.concatenate([q5[:, :, :, 0], z], -1), jnp.concatenate([z, q5[:, :, :, 1]], -1)], axis=1)
    qall = qall.reshape(db, DECODE_ROWS, LANES)
    per_row = lambda t: jnp.broadcast_to(t.reshape(db, 1, KV_HEADS, 1, LANES),
                                         (db, 2, KV_HEADS, KV_REP, LANES)).reshape(db, DECODE_ROWS, LANES)
    kern = functools.partial(_attn_decode_kernel, pages=pages, lam0=lam0)
    row = lambda b, p, pt: (b, 0, 0)
    page = lambda j: (lambda b, p, pt: (layer, pt[b, p * pages + j], 0, 0))
    grid_spec = pltpu.PrefetchScalarGridSpec(
        num_scalar_prefetch=1,
        grid=(db, n_pages // pages),
        in_specs=_lam_specs() + [pl.BlockSpec((1, DECODE_ROWS, LANES), row) for _ in range(3)]
        + [pl.BlockSpec((1, 1, page_rows, 2 * HEAD_DIM), page(j)) for j in range(pages)]
        + [pl.BlockSpec((1, 1, page_rows, V_DIM), page(j)) for j in range(pages)],
        out_specs=pl.BlockSpec((1, DECODE_ROWS // 2, V_DIM), row),
        scratch_shapes=[pltpu.VMEM((DECODE_ROWS, 1), F32), pltpu.VMEM((DECODE_ROWS, 1), F32),
                        pltpu.VMEM((DECODE_ROWS, V_DIM), F32)],
    )
    out = pl.pallas_call(
        kern,
        grid_spec=grid_spec,
        out_shape=jax.ShapeDtypeStruct((db, DECODE_ROWS // 2, V_DIM), F32),
        compiler_params=_params("arbitrary", "arbitrary"),
        name="attn_decode",
    )(page_table, *lam_vecs, subln_g.reshape(1, V_DIM), qall, per_row(ks), per_row(vs),
      *([ck] * pages), *([cv] * pages))
    return out.reshape(db, ATT_WIDTH)


def _gate_norm(y, xs_c, z, d_e, g_e):
    y = (y + d_e * xs_c) * _silu(z)
    outs = []
    for g in range(SSM_GROUPS):
        yg = y[:, g * SSM_GROUP_W:(g + 1) * SSM_GROUP_W]
        ms = jnp.mean(yg * yg, axis=-1, keepdims=True)
        outs.append(yg * lax.rsqrt(ms + EPS) * g_e[:, g * SSM_GROUP_W:(g + 1) * SSM_GROUP_W])
    return outs


def _dt_raw(xg, rs, wdt_ref):
    w = wdt_ref[0].astype(BF16)
    w = jnp.concatenate([w, jnp.zeros((LANES - SSM_HEADS, w.shape[1]), BF16)], axis=0)
    return lax.dot_general(xg, w, (((1,), (1,)), ((), ())), preferred_element_type=F32) * rs


def _dt_weight_spec(layer):
    return pl.BlockSpec((1, SSM_HEADS, D_MODEL), lambda *a: (layer, DT_OFF // SSM_HEADS, 0))


def _ssd_prompt_kernel(z_ref, x_ref, b_ref, c_ref, xg_ref, rs_ref, wdt_ref, cwx, cwb, cwc, cbx, cbb, cbc, dtb_c, alog_c,
                       d_e, g_e, tril_ref, exp_ref, tail_ref, o_ref, s_ref, xpx, xpb, xpc, st_sc, y_sc):
    c = pl.program_id(1)
    q = SSD_CHUNK

    @pl.when(c == 0)
    def _():
        xpx[0:8, :] = jnp.zeros((8, xpx.shape[1]), F32)
        xpb[0:8, :] = jnp.zeros((8, xpb.shape[1]), F32)
        xpc[0:8, :] = jnp.zeros((8, xpc.shape[1]), F32)
        st_sc[...] = jnp.zeros(st_sc.shape, F32)

    def conv(src, pad, w, b):
        pad[8:8 + q, :] = src[...]
        y = b[...]
        for t in range(CONV_WIDTH):
            y = y + w[t:t + 1, :] * pad[5 + t:5 + t + q, :]
        tail = pad[q:q + 8, :]
        pad[0:8, :] = tail
        return _silu(y)

    xs_c = conv(x_ref, xpx, cwx, cbx)
    bc = conv(b_ref, xpb, cwb, cbb)
    cc = conv(c_ref, xpc, cwc, cbc)

    tril = tril_ref[...]
    dtr = _dt_raw(xg_ref[...], rs_ref[...], wdt_ref)
    dt_c = _softplus(dtr + dtb_c[...])
    acs_c = _dot01_lhs(tril, dt_c * (-jnp.exp(alog_c[...])))
    acs_t = acs_c.T
    dt_e = _dot01_rhs(dt_c, exp_ref[...])
    acs_e = _dot01_rhs(acs_c, exp_ref[...])
    a_last = acs_e[q - 1:q, :]
    xdt = xs_c * dt_e
    xdtd = (xdt * jnp.exp(a_last - acs_e)).astype(BF16)
    xdt = xdt.astype(BF16)
    ea = jnp.exp(acs_e)
    dec_end = jnp.exp(a_last)

    ii = lax.broadcasted_iota(jnp.int32, (q, q), 0)
    jj = lax.broadcasted_iota(jnp.int32, (q, q), 1)
    causal = jj <= ii
    lane_lo = jj < SSM_HEAD_DIM

    for g in range(SSM_GROUPS):
        gs = slice(g * SSM_GROUP_W, (g + 1) * SSM_GROUP_W)
        bg = bc[:, g * SSM_STATE:(g + 1) * SSM_STATE].astype(BF16)
        cg = cc[:, g * SSM_STATE:(g + 1) * SSM_STATE].astype(BF16)
        cb = lax.dot_general(cg, bg, (((1,), (1,)), ((), ())), preferred_element_type=F32)
        st = st_sc[g]
        y_off = _dot(cg, st.astype(BF16)) * ea[:, gs]
        pairs = []
        for pr in range(SSM_HPG // 2):
            col = g * SSM_GROUP_W + pr * LANES
            xpair = xdt[:, col:col + LANES]
            ys = []
            for hh in range(2):
                head = g * SSM_HPG + pr * 2 + hh
                seg = acs_c[:, head:head + 1] - acs_t[head:head + 1, :]
                lmat = jnp.where(causal, jnp.exp(seg), 0.0)
                ys.append(_dot((cb * lmat).astype(BF16), xpair))
            pairs.append(jnp.where(lane_lo, ys[0], ys[1]))
        y_sc[:, gs] = jnp.concatenate(pairs, axis=1) + y_off
        upd = lax.dot_general(bg, xdtd[:, gs], (((0,), (0,)), ((), ())), preferred_element_type=F32)
        st_sc[g] = st * dec_end[:, gs] + upd

    outs = _gate_norm(y_sc[...], xs_c, z_ref[...], d_e[...], g_e[...])
    for g in range(SSM_GROUPS):
        o_ref[:, g * SSM_GROUP_W:(g + 1) * SSM_GROUP_W] = outs[g].astype(o_ref.dtype)

    @pl.when(c == pl.num_programs(1) - 1)
    def _():
        for g in range(SSM_GROUPS):
            t = st_sc[g].T
            for r in range(SSM_HPG):
                s_ref[0, g * SSM_HPG + r] = t[r * SSM_HEAD_DIM:(r + 1) * SSM_HEAD_DIM, :]


def _ssm_consts(conv_w, conv_b, dt_bias, a_log, d_skip, ssm_g):
    rep = lambda v: jnp.repeat(v.astype(F32), SSM_HEAD_DIM).reshape(1, SSM_WIDTH)
    pad = lambda v: jnp.pad(v.astype(F32), (0, LANES - SSM_HEADS)).reshape(1, LANES)
    head_of_col = jnp.arange(SSM_WIDTH) // SSM_HEAD_DIM
    expand = (jnp.arange(LANES)[:, None] == head_of_col[None, :]).astype(BF16)
    tril = (jnp.arange(SSD_CHUNK)[None, :] <= jnp.arange(SSD_CHUNK)[:, None]).astype(BF16)
    return dict(conv_w=conv_w.astype(F32), conv_b=conv_b.astype(F32).reshape(1, CONV_CH),
                dtb_c=pad(dt_bias), alog_c=pad(a_log),
                d_e=rep(d_skip), g_e=ssm_g.astype(F32).reshape(1, SSM_WIDTH), tril=tril, expand=expand)


def _conv_param_specs(zmap):
    n_x, n_b = SSM_WIDTH, SSM_GROUPS * SSM_STATE
    cmap = lambda k: (lambda *a: (0, k))
    return [pl.BlockSpec((CONV_WIDTH, n_x), cmap(0)), pl.BlockSpec((CONV_WIDTH, n_b), cmap(2)),
            pl.BlockSpec((CONV_WIDTH, n_b), cmap(3)),
            pl.BlockSpec((1, n_x), cmap(0)), pl.BlockSpec((1, n_b), cmap(2)), pl.BlockSpec((1, n_b), cmap(3))]


def _ssd_prompt(h, xg, rs, w_dt, layer, sc, batch, seq, tail):
    q = SSD_CHUNK
    nc = seq // q
    rb = lambda b, c: b * nc + c
    n_b = SSM_GROUPS * SSM_STATE
    zmap = lambda b, c: (0, 0)
    vec = lambda n: pl.BlockSpec((1, n), zmap)
    in_specs = [
        pl.BlockSpec((q, SSM_WIDTH), lambda b, c: (rb(b, c), Z_OFF // SSM_WIDTH)),
        pl.BlockSpec((q, SSM_WIDTH), lambda b, c: (rb(b, c), X_OFF // SSM_WIDTH)),
        pl.BlockSpec((q, n_b), lambda b, c: (rb(b, c), B_OFF // n_b)),
        pl.BlockSpec((q, n_b), lambda b, c: (rb(b, c), C_OFF // n_b)),
        pl.BlockSpec((q, D_MODEL), lambda b, c: (rb(b, c), 0)),
        pl.BlockSpec((q, LANES), lambda b, c: (rb(b, c), 0)),
        _dt_weight_spec(layer),
    ] + _conv_param_specs(zmap) + [
        vec(LANES), vec(LANES), vec(SSM_WIDTH), vec(SSM_WIDTH),
        pl.BlockSpec((q, q), zmap), pl.BlockSpec((LANES, SSM_WIDTH), zmap),
        pl.BlockSpec(memory_space=pl.ANY),
    ]
    out, state = pl.pallas_call(
        _ssd_prompt_kernel,
        grid=(batch, nc),
        in_specs=in_specs,
        out_specs=[pl.BlockSpec((q, SSM_WIDTH), lambda b, c: (rb(b, c), 0)),
                   pl.BlockSpec((1, SSM_HEADS, SSM_HEAD_DIM, SSM_STATE), lambda b, c: (b, 0, 0, 0))],
        out_shape=[jax.ShapeDtypeStruct(tail.shape, BF16),
                   jax.ShapeDtypeStruct((batch, SSM_HEADS, SSM_HEAD_DIM, SSM_STATE), F32)],
        scratch_shapes=[pltpu.VMEM((q + 8, SSM_WIDTH), F32), pltpu.VMEM((q + 8, n_b), F32),
                        pltpu.VMEM((q + 8, n_b), F32),
                        pltpu.VMEM((SSM_GROUPS, SSM_STATE, SSM_GROUP_W), F32), pltpu.VMEM((q, SSM_WIDTH), F32)],
        input_output_aliases={len(in_specs) - 1: 0},
        compiler_params=_params("arbitrary", "arbitrary"),
        name="ssd_prompt",
    )(h, h, h, h, xg, rs, w_dt, sc["conv_w"], sc["conv_w"], sc["conv_w"], sc["conv_b"], sc["conv_b"], sc["conv_b"],
      sc["dtb_c"], sc["alog_c"], sc["d_e"], sc["g_e"], sc["tril"], sc["expand"], tail)
    return out, state


def _ssm_decode_pre_kernel(x_ref, b_ref, c_ref, xg_ref, rs_ref, wdt_ref, prev_ref, cwx, cwb, cwc, cbx, cbb, cbc, dtb_c, alog_c,
                           exp_ref, xs_o, b_o, c_o, xdt_o, dec_o):
    def conv(src, col, w, b):
        n = src.shape[1]
        y = b[...] + w[CONV_WIDTH - 1:CONV_WIDTH, :] * src[...]
        for t in range(CONV_WIDTH - 1):
            y = y + w[t:t + 1, :] * prev_ref[t, :, col:col + n]
        return _silu(y)

    xs_c = conv(x_ref, 0, cwx, cbx)
    xs_o[...] = xs_c
    b_o[...] = conv(b_ref, SSM_WIDTH, cwb, cbb)
    c_o[...] = conv(c_ref, SSM_WIDTH + SSM_GROUPS * SSM_STATE, cwc, cbc)
    dtr = _dt_raw(xg_ref[...], rs_ref[...], wdt_ref)[:x_ref.shape[0]]
    dt_c = _softplus(dtr + dtb_c[...])
    dec_o[...] = jnp.exp(dt_c * (-jnp.exp(alog_c[...])))
    xdt_o[...] = xs_c * _dot01_rhs(dt_c, exp_ref[...])


def _ssm_decode_state_kernel(s_ref, xdt_ref, dec_ref, b_ref, c_ref, so_ref, y_ref):
    for head in range(SSM_HEADS):
        g = head // SSM_HPG
        s = dec_ref[0, head] * s_ref[0, 0, head] + xdt_ref[0, head] * b_ref[0, g]
        so_ref[0, head] = s
        y_ref[0, head] = jnp.sum(s * c_ref[0, g], axis=-1, keepdims=True)


def _ssm_decode_post_kernel(y_ref, xs_ref, z_ref, d_e, g_e, o_ref):
    outs = _gate_norm(y_ref[...], xs_ref[...], z_ref[...], d_e[...], g_e[...])
    o_ref[...] = jnp.zeros(o_ref.shape, o_ref.dtype)
    db = y_ref.shape[0]
    for g in range(SSM_GROUPS):
        o_ref[0:db, g * SSM_GROUP_W:(g + 1) * SSM_GROUP_W] = outs[g]


def _ssm_decode(h, xg, rs, w_dt, sc, state_conv_l, state_ssm, layer, row0, db):
    n_b = SSM_GROUPS * SSM_STATE
    rblk = row0 // db
    zmap = lambda i: (0, 0)
    vec = lambda n: pl.BlockSpec((1, n), zmap)
    prev = jnp.transpose(state_conv_l, (1, 0, 2)).astype(F32)
    f = lambda n: jax.ShapeDtypeStruct((db, n), F32)
    xs_c, b_c, c_c, xdt, dec = pl.pallas_call(
        _ssm_decode_pre_kernel,
        grid=(1,),
        in_specs=[
            pl.BlockSpec((db, SSM_WIDTH), lambda i: (rblk, X_OFF // SSM_WIDTH)),
            pl.BlockSpec((db, n_b), lambda i: (rblk, B_OFF // n_b)),
            pl.BlockSpec((db, n_b), lambda i: (rblk, C_OFF // n_b)),
            pl.BlockSpec((PAD_ROWS, D_MODEL), lambda i: (row0 // PAD_ROWS, 0)),
            pl.BlockSpec((PAD_ROWS, LANES), lambda i: (row0 // PAD_ROWS, 0)),
            _dt_weight_spec(layer),
            pl.BlockSpec((CONV_WIDTH - 1, db, CONV_CH), lambda i: (0, 0, 0)),
        ] + _conv_param_specs(zmap) + [vec(LANES), vec(LANES),
                                       pl.BlockSpec((LANES, SSM_WIDTH), zmap)],
        out_specs=[pl.BlockSpec((db, SSM_WIDTH), zmap), pl.BlockSpec((db, n_b), zmap), pl.BlockSpec((db, n_b), zmap),
                   pl.BlockSpec((db, SSM_WIDTH), zmap), pl.BlockSpec((db, LANES), zmap)],
        out_shape=[f(SSM_WIDTH), f(n_b), f(n_b), f(SSM_WIDTH), f(LANES)],
        compiler_params=_params("arbitrary"),
        name="ssm_decode_pre",
    )(h, h, h, xg, rs, w_dt, prev, sc["conv_w"], sc["conv_w"], sc["conv_w"], sc["conv_b"], sc["conv_b"], sc["conv_b"],
      sc["dtb_c"], sc["alog_c"], sc["expand"])

    xdt4 = xdt.reshape(db, SSM_HEADS, SSM_HEAD_DIM, 1)
    dec4 = dec[:, :SSM_HEADS].reshape(db, SSM_HEADS, 1, 1)
    b4 = b_c.reshape(db, SSM_GROUPS, 1, SSM_STATE)
    c4 = c_c.reshape(db, SSM_GROUPS, 1, SSM_STATE)
    s_new, y4 = pl.pallas_call(
        _ssm_decode_state_kernel,
        grid=(db,),
        in_specs=[
            pl.BlockSpec((1, 1, SSM_HEADS, SSM_HEAD_DIM, SSM_STATE), lambda b: (layer, b, 0, 0, 0)),
            pl.BlockSpec((1, SSM_HEADS, SSM_HEAD_DIM, 1), lambda b: (b, 0, 0, 0)),
            pl.BlockSpec((1, SSM_HEADS, 1, 1), lambda b: (b, 0, 0, 0)),
            pl.BlockSpec((1, SSM_GROUPS, 1, SSM_STATE), lambda b: (b, 0, 0, 0)),
            pl.BlockSpec((1, SSM_GROUPS, 1, SSM_STATE), lambda b: (b, 0, 0, 0)),
        ],
        out_specs=[pl.BlockSpec((1, SSM_HEADS, SSM_HEAD_DIM, SSM_STATE), lambda b: (b, 0, 0, 0)),
                   pl.BlockSpec((1, SSM_HEADS, SSM_HEAD_DIM, 1), lambda b: (b, 0, 0, 0))],
        out_shape=[jax.ShapeDtypeStruct((db, SSM_HEADS, SSM_HEAD_DIM, SSM_STATE), F32),
                   jax.ShapeDtypeStruct((db, SSM_HEADS, SSM_HEAD_DIM, 1), F32)],
        compiler_params=_params("arbitrary"),
        name="ssm_decode_state",
    )(state_ssm, xdt4, dec4, b4, c4)

    y = y4.reshape(db, SSM_WIDTH)
    out = pl.pallas_call(
        _ssm_decode_post_kernel,
        grid=(1,),
        in_specs=[pl.BlockSpec((db, SSM_WIDTH), zmap), pl.BlockSpec((db, SSM_WIDTH), zmap),
                  pl.BlockSpec((db, SSM_WIDTH), lambda i: (rblk, Z_OFF // SSM_WIDTH)),
                  vec(SSM_WIDTH), vec(SSM_WIDTH)],
        out_specs=pl.BlockSpec((PAD_ROWS, SSM_WIDTH), zmap),
        out_shape=jax.ShapeDtypeStruct((PAD_ROWS, SSM_WIDTH), F32),
        compiler_params=_params("arbitrary"),
        name="ssm_decode_post",
    )(y, xs_c, h, sc["d_e"], sc["g_e"])
    return out, s_new


def _lambda_init(layer):
    return 0.8 - 0.6 * math.exp(-0.3 * layer)


def _token_mixing(x, xg, rs, layer, batch, seq, db, w, cache_k, cache_v, state_conv, state_ssm, page_table):
    n_prompt = batch * seq
    lam0 = _lambda_init(layer)
    w_dt = w["w_in_t"]
    h = _matmul("in_proj", layer, [(xg, D_MODEL, 0)], [(w_dt, D_MODEL, 0)], [[(0, 0)]], _epi_scaled, [],
                DT_OFF, F32, tn=512, tm_cap=1024, w_trans=(True,), row_scale=rs)
    qn, kn = _qk_prep(h, w["q_norm"], w["k_norm"])
    lam_vecs = [w[k].astype(F32).reshape(1, HEAD_DIM) for k in ("lambda_q1", "lambda_k1", "lambda_q2", "lambda_k2")]
    dec = slice(n_prompt, n_prompt + db)
    v_off = Q_COLS + K_COLS

    sc = _ssm_consts(w["conv_w"], w["conv_b"], w["dt_bias"], w["a_log"], w["d_skip"], w["ssm_norm"])

    qs = qn[dec].astype(F32)
    ks = kn[dec]
    vs = h[dec, v_off:v_off + V_COLS]
    att_s = _attn_decode(qs, ks, vs, cache_k, cache_v, page_table, layer, lam_vecs, w["attn_subln"], lam0)
    ssm_s, ssm_state_s = _ssm_decode(h, xg, rs, w_dt, sc, state_conv[layer], state_ssm, layer, n_prompt, db)

    below = lambda t: jnp.pad(t.astype(BF16), ((n_prompt, PAD_ROWS - t.shape[0]), (0, 0)))
    att = _attn_prompt(qn, kn, h, lam_vecs, w["attn_subln"], lam0, batch, seq, below(att_s))
    ssm, ssm_state_p = _ssd_prompt(h, xg, rs, w_dt, layer, sc, batch, seq, below(ssm_s))

    half = ATT_WIDTH
    x, xg, rs = _matmul("out_proj", layer, [(att, half, 0), (ssm, half, 0)],
                        [(w["w_out"], half, 0), (w["w_out"], half, 1)], [[(0, 0), (1, 1)]], _epi_residual, [x],
                        D_MODEL, F32, tn=512, tm_cap=1024, next_gain=w["ffn2_norm"])

    k_p = kn[:n_prompt].reshape(batch, seq, KV_HEADS, 2 * HEAD_DIM)
    v_p = h[:n_prompt, v_off:v_off + V_COLS].reshape(batch, seq, KV_HEADS, V_DIM)
    conv_p = jnp.stack([h[(b + 1) * seq - (CONV_WIDTH - 1):(b + 1) * seq, X_OFF:X_OFF + CONV_CH] for b in range(batch)])
    k_s = ks.reshape(db, 1, KV_HEADS, 2 * HEAD_DIM)
    v_s = vs.reshape(db, 1, KV_HEADS, V_DIM)
    conv_s = jnp.concatenate([state_conv[layer][:, 1:].astype(F32), h[dec, X_OFF:X_OFF + CONV_CH][:, None, :]], axis=1)
    return x, xg, rs, (k_p, v_p, conv_p, ssm_state_p, k_s, v_s, conv_s, ssm_state_s)


def _ple(x, xg, rs, layer, next_gain, p, w_gate, w_proj):
    return _matmul("ple", layer, [(xg, D_MODEL, 0), (p, PLE_DIM, 0)], [(w_gate, D_MODEL, 0), (w_proj, PLE_DIM, 0)],
                   [[(0, 0)], [(1, 1)]], _epi_gated, [x], D_MODEL, F32, tn=512, tm_cap=1024, row_scale=rs,
                   next_gain=next_gain)


def kernel(x_prompt, x_sample, cache_k, cache_v, state_conv, state_ssm, page_table, p_prompt, p_sample, ffn1_norm, ffn1_w_gate, ffn1_w_up, ffn1_w_down, mix_norm, w_in, q_norm, k_norm, lambda_q1, lambda_k1, lambda_q2, lambda_k2, attn_subln, conv_w, conv_b, dt_bias, a_log, d_skip, ssm_norm, w_out, ffn2_norm, ffn2_w_gate, ffn2_w_up, ffn2_w_down, ple_norm, ple_w_gate, ple_w_proj):
    batch, seq, d = x_prompt.shape
    db = x_sample.shape[0]
    depth = w_in.shape[0]
    n_prompt = batch * seq
    x = jnp.concatenate([x_prompt.reshape(n_prompt, d), x_sample.reshape(db, d),
                         jnp.zeros((PAD_ROWS - db, d), F32)], axis=0)
    p_all = jnp.concatenate([p_prompt.reshape(depth, n_prompt, PLE_DIM), p_sample.reshape(depth, db, PLE_DIM),
                             jnp.zeros((depth, PAD_ROWS - db, PLE_DIM), F32)], axis=1).astype(BF16)
    w_in_t = jnp.swapaxes(w_in, 1, 2)
    outs = []
    xg, rs = _norm_factors(x, ffn1_norm[0])
    for i in range(depth):
        x, xg, rs = _ffn(x, xg, rs, i, mix_norm[i], ffn1_w_gate, ffn1_w_up, ffn1_w_down)
        w = dict(ffn2_norm=ffn2_norm[i], w_in_t=w_in_t,
                 q_norm=q_norm[i], k_norm=k_norm[i], lambda_q1=lambda_q1[i], lambda_k1=lambda_k1[i],
                 lambda_q2=lambda_q2[i], lambda_k2=lambda_k2[i], attn_subln=attn_subln[i], conv_w=conv_w[i],
                 conv_b=conv_b[i], dt_bias=dt_bias[i], a_log=a_log[i], d_skip=d_skip[i], ssm_norm=ssm_norm[i],
                 w_out=w_out)
        x, xg, rs, o = _token_mixing(x, xg, rs, i, batch, seq, db, w, cache_k, cache_v, state_conv, state_ssm,
                                     page_table)
        outs.append(o)
        x, xg, rs = _ffn(x, xg, rs, i, ple_norm[i], ffn2_w_gate, ffn2_w_up, ffn2_w_down)
        if i + 1 < depth:
            x, xg, rs = _ple(x, xg, rs, i, ffn1_norm[i + 1], p_all[i], ple_w_gate, ple_w_proj)
        else:
            x = _ple(x, xg, rs, i, None, p_all[i], ple_w_gate, ple_w_proj)
    stacked = [jnp.stack([o[k] for o in outs]) for k in range(8)]
    y_prompt = x[:n_prompt].reshape(batch, seq, d)
    y_sample = x[n_prompt:n_prompt + db].reshape(db, 1, d)
    return (y_prompt, y_sample, *stacked)
```
